```python
import jax, jax.numpy as jnp
from jax import lax
import numpy as np

D_MODEL = 4096
BATCH = 4
SEQ = 2048
DEPTH = 1
DEC_BATCH = 128
DEC_SEQ = 4
PAST_LEN = 16384
PAGE_SIZE = 128

CHUNK = 128
HEAD_DIM_A = 128
HEADS_A = D_MODEL // (2 * HEAD_DIM_A)
CH_A = HEADS_A * HEAD_DIM_A
POOL_WINDOWS = (2, 4, 8, 16)
POOL_GROUPS = len(POOL_WINDOWS)
CH_B = D_MODEL // 2
POOL_GC = CH_B // POOL_GROUPS
POOL_BUF = max(POOL_WINDOWS) - 1
IN_W = 2 * CH_A + CH_B + 2 * D_MODEL
N_EXPERTS = 256
TOP_K = 8
N_GROUPS = 8
TOPK_GROUPS = 4
D_EXPERT = 512
D_SHARED = 512
ROUTE_SCALE = 2.5
MOE_BLOCK = 64
PLE_DIM = 256
ALPHA = (2.0 * DEPTH) ** 0.25
BETA = (8.0 * DEPTH) ** -0.25
LN_EPS = 1e-5

kernel_name = 'hybrid_gmlp_pool_moe_deepnorm_step'


def layer_norm(x, g, b):
    xf = x.astype(jnp.float32)
    mu = jnp.mean(xf, axis=-1, keepdims=True)
    xc = xf - mu
    var = jnp.mean(xc * xc, axis=-1, keepdims=True)
    return (xc * lax.rsqrt(var + LN_EPS) * g.astype(jnp.float32) + b.astype(jnp.float32)).astype(x.dtype)


def chunk_spatial_mix(vn, w_s, b_s):
    bsz, t, _ = vn.shape
    n_chunks = -(-t // CHUNK)
    pad = n_chunks * CHUNK - t
    vp = jnp.pad(vn, ((0, 0), (0, pad), (0, 0))).reshape(bsz, n_chunks, CHUNK, HEADS_A, HEAD_DIM_A)
    causal = jnp.tril(jnp.ones((CHUNK, CHUNK), dtype=bool))
    w = jnp.where(causal[None], w_s, jnp.zeros((), w_s.dtype))
    s = jnp.einsum('hij,bcjhd->bcihd', w, vp) + b_s.T[:, :, None]
    return s.reshape(bsz, n_chunks * CHUNK, CH_A)[:, :t]


def multiscale_pool(z, buf, pos0, w_pool, pool_scale):
    bsz, t, _ = z.shape
    cat = jnp.concatenate([buf, z], axis=1)
    cs = jnp.cumsum(cat.astype(jnp.float32), axis=1)
    cs = jnp.pad(cs, ((0, 0), (1, 0), (0, 0)))
    pos = pos0 + jnp.arange(t)
    means = []
    for g, win in enumerate(POOL_WINDOWS):
        sl = slice(g * POOL_GC, (g + 1) * POOL_GC)
        wsum = cs[:, POOL_BUF + 1:POOL_BUF + 1 + t, sl] - cs[:, POOL_BUF + 1 - win:POOL_BUF + 1 - win + t, sl]
        cnt = jnp.minimum(pos + 1, win).astype(jnp.float32)[None, :, None]
        means.append(wsum / cnt)
    mean = jnp.concatenate(means, axis=-1)
    d = (mean - z.astype(jnp.float32)).astype(z.dtype).reshape(bsz, t, POOL_GROUPS, POOL_GC)
    y = jnp.einsum('btgc,gcd->btgd', d, w_pool).reshape(bsz, t, CH_B) * pool_scale
    return y, cat[:, -POOL_BUF:]


def swiglu(x, w1, w3, w2):
    return (jax.nn.silu(x @ w1) * (x @ w3)) @ w2


def routed_experts(xt, eidx, gw, w1, w3, w2):
    t = xt.shape[0]
    n = t * TOP_K
    n_blocks = -(-n // MOE_BLOCK) + N_EXPERTS
    flat_e = eidx.reshape(-1)
    flat_tok = jnp.repeat(jnp.arange(t, dtype=jnp.int32), TOP_K)
    flat_w = gw.reshape(-1)
    order = jnp.argsort(flat_e)
    se, stok, sw = flat_e[order], flat_tok[order], flat_w[order]
    counts = jnp.bincount(flat_e, length=N_EXPERTS)
    padded = (counts + MOE_BLOCK - 1) // MOE_BLOCK * MOE_BLOCK
    pend = jnp.cumsum(padded)
    pstart = pend - padded
    sstart = jnp.cumsum(counts) - counts
    ppos = pstart[se] + jnp.arange(n) - sstart[se]
    pad_tok = jnp.zeros((n_blocks * MOE_BLOCK,), jnp.int32).at[ppos].set(stok)
    pad_w = jnp.zeros((n_blocks * MOE_BLOCK,), xt.dtype).at[ppos].set(sw)
    blk_e = jnp.minimum(jnp.searchsorted(pend, jnp.arange(n_blocks) * MOE_BLOCK, side='right'), N_EXPERTS - 1)

    def step(acc, blk):
        tok, wt, e = blk
        xb = xt[tok]
        yb = swiglu(xb, w1[e], w3[e], w2[e]) * wt[:, None]
        return acc.at[tok].add(yb), None

    acc, _ = lax.scan(step, jnp.zeros_like(xt),
                      (pad_tok.reshape(n_blocks, MOE_BLOCK), pad_w.reshape(n_blocks, MOE_BLOCK), blk_e))
    return acc


def moe_ffn(xt, w_router, b_router, w1, w3, w2, ws1, ws3, ws2):
    t = xt.shape[0]
    scores = jax.nn.sigmoid((xt @ w_router).astype(jnp.float32))
    biased = scores + b_router.astype(jnp.float32)
    per_group = N_EXPERTS // N_GROUPS
    gscore = lax.top_k(biased.reshape(t, N_GROUPS, per_group), 2)[0].sum(-1)
    _, gidx = lax.top_k(gscore, TOPK_GROUPS)
    gmask = jax.nn.one_hot(gidx, N_GROUPS).sum(1) > 0
    masked = jnp.where(jnp.repeat(gmask, per_group, axis=1), biased, -jnp.inf)
    _, eidx = lax.top_k(masked, TOP_K)
    gw = jnp.take_along_axis(scores, eidx, axis=1)
    gw = gw / jnp.sum(gw, axis=-1, keepdims=True) * ROUTE_SCALE
    routed = routed_experts(xt, eidx.astype(jnp.int32), gw.astype(xt.dtype), w1, w3, w2)
    return routed + swiglu(xt, ws1, ws3, ws2)


def trunk_layer(x, p_l, pool_buf, pos0, w_in, ln_v_g, ln_v_b, w_s, b_s, w_pool, pool_scale,
                w_up_a, w_up_b, w_o, ln_g, ln_b, w_router, b_router, w1, w3, w2, ws1, ws3, ws2,
                w_ple_gate, w_ple_proj):
    bsz, t, _ = x.shape
    h = x @ w_in
    i1, i2, i3 = CH_A, 2 * CH_A, 2 * CH_A + CH_B
    u = jax.nn.gelu(h[..., :i1])
    v = jax.nn.gelu(h[..., i1:i2])
    z = h[..., i2:i3]
    g_a = jax.nn.sigmoid(h[..., i3:i3 + D_MODEL])
    g_b = jax.nn.sigmoid(h[..., i3 + D_MODEL:])
    vn = layer_norm(v, ln_v_g, ln_v_b)
    o_a = u * chunk_spatial_mix(vn, w_s, b_s)
    o_b, new_buf = multiscale_pool(z, pool_buf, pos0, w_pool, pool_scale)
    merged = g_a * (o_a @ w_up_a) + g_b * (o_b @ w_up_b)
    x1 = layer_norm(ALPHA * x + merged @ w_o, ln_g[0], ln_b[0])
    ffn = moe_ffn(x1.reshape(bsz * t, D_MODEL), w_router, b_router, w1, w3, w2, ws1, ws3, ws2)
    x2 = layer_norm(ALPHA * x1 + ffn.reshape(bsz, t, D_MODEL), ln_g[1], ln_b[1])
    pe = jax.nn.sigmoid(x2 @ w_ple_gate) * (p_l @ w_ple_proj)
    x3 = layer_norm(ALPHA * x2 + pe, ln_g[2], ln_b[2])
    return x3, new_buf, vn


def setup_inputs(seed: int = 0) -> dict:
    key = jax.random.key(seed)
    ks = jax.random.split(key, 27)

    def nrm(k, shape, scale):
        return jax.random.normal(k, shape, jnp.float32) * scale

    return {
        'x_prompt': nrm(ks[0], (BATCH, SEQ, D_MODEL), 1.0),
        'x_sample': nrm(ks[1], (DEC_BATCH, DEC_SEQ, D_MODEL), 1.0),
        'state_pool': nrm(ks[2], (DEPTH, DEC_BATCH, POOL_BUF, CH_B), 1.0),
        'p_prompt': nrm(ks[3], (DEPTH, BATCH, SEQ, PLE_DIM), 1.0),
        'p_sample': nrm(ks[4], (DEPTH, DEC_BATCH, DEC_SEQ, PLE_DIM), 1.0),
        'w_in': nrm(ks[5], (DEPTH, D_MODEL, IN_W), D_MODEL ** -0.5),
        'ln_v_g': 1.0 + nrm(ks[6], (DEPTH, CH_A), 0.02),
        'ln_v_b': nrm(ks[7], (DEPTH, CH_A), 0.02),
        'w_s': nrm(ks[8], (DEPTH, HEADS_A, CHUNK, CHUNK), CHUNK ** -0.5),
        'b_s': 1.0 + nrm(ks[9], (DEPTH, HEADS_A, CHUNK), 0.01),
        'w_pool': nrm(ks[10], (DEPTH, POOL_GROUPS, POOL_GC, POOL_GC), POOL_GC ** -0.5),
        'pool_scale': 1.0 + nrm(ks[11], (DEPTH, CH_B), 0.02),
        'w_up_a': nrm(ks[12], (DEPTH, CH_A, D_MODEL), CH_A ** -0.5),
        'w_up_b': nrm(ks[13], (DEPTH, CH_B, D_MODEL), CH_B ** -0.5),
        'w_o': nrm(ks[14], (DEPTH, D_MODEL, D_MODEL), BETA * D_MODEL ** -0.5),
        'ln_g': 1.0 + nrm(ks[15], (DEPTH, 3, D_MODEL), 0.02),
        'ln_b': nrm(ks[16], (DEPTH, 3, D_MODEL), 0.02),
        'w_router': nrm(ks[17], (DEPTH, D_MODEL, N_EXPERTS), D_MODEL ** -0.5),
        'b_router': nrm(ks[18], (DEPTH, N_EXPERTS), 0.01),
        'w1': nrm(ks[19], (DEPTH, N_EXPERTS, D_MODEL, D_EXPERT), D_MODEL ** -0.5),
        'w3': nrm(ks[20], (DEPTH, N_EXPERTS, D_MODEL, D_EXPERT), D_MODEL ** -0.5),
        'w2': nrm(ks[21], (DEPTH, N_EXPERTS, D_EXPERT, D_MODEL), BETA * D_EXPERT ** -0.5),
        'ws1': nrm(ks[22], (DEPTH, D_MODEL, D_SHARED), D_MODEL ** -0.5),
        'ws3': nrm(ks[23], (DEPTH, D_MODEL, D_SHARED), D_MODEL ** -0.5),
        'ws2': nrm(ks[24], (DEPTH, D_SHARED, D_MODEL), BETA * D_SHARED ** -0.5),
        'w_ple_gate': nrm(ks[25], (DEPTH, D_MODEL, D_MODEL), D_MODEL ** -0.5),
        'w_ple_proj': nrm(ks[26], (DEPTH, PLE_DIM, D_MODEL), BETA * PLE_DIM ** -0.5),
    }


def reference(x_prompt, x_sample, state_pool, p_prompt, p_sample, w_in, ln_v_g, ln_v_b, w_s, b_s,
              w_pool, pool_scale, w_up_a, w_up_b, w_o, ln_g, ln_b, w_router, b_router, w1, w3, w2,
              ws1, ws3, ws2, w_ple_gate, w_ple_proj):
    y_p, y_s = x_prompt, x_sample
    pool_p, pool_s, chunk_v_s = [], [], []
    for i in range(DEPTH):
        lw = (w_in[i], ln_v_g[i], ln_v_b[i], w_s[i], b_s[i], w_pool[i], pool_scale[i], w_up_a[i], w_up_b[i],
              w_o[i], ln_g[i], ln_b[i], w_router[i], b_router[i], w1[i], w3[i], w2[i], ws1[i], ws3[i], ws2[i],
              w_ple_gate[i], w_ple_proj[i])
        empty_buf = jnp.zeros((y_p.shape[0], POOL_BUF, CH_B), y_p.dtype)
        y_p, buf_p, _ = trunk_layer(y_p, p_prompt[i], empty_buf, 0, *lw)
        y_s, buf_s, vn_s = trunk_layer(y_s, p_sample[i], state_pool[i], PAST_LEN, *lw)
        pool_p.append(buf_p)
        pool_s.append(buf_s)
        chunk_v_s.append(vn_s)
    new_pool_prompt = jnp.stack(pool_p, axis=0)
    new_pool_sample = jnp.stack(pool_s, axis=0)
    new_chunk_v_sample = jnp.stack(chunk_v_s, axis=0)
    return (y_p, y_s, new_pool_prompt, new_pool_sample, new_chunk_v_sample)
```

```python
import functools

import jax
import jax.numpy as jnp
from jax import lax
from jax.experimental import pallas as pl
from jax.experimental.pallas import tpu as pltpu

F32 = jnp.float32
BF16 = jnp.bfloat16

D_MODEL = 4096
BATCH = 4
SEQ = 2048
DEC_BATCH = 128
DEC_SEQ = 4
PAST_LEN = 16384
CHUNK = 128
HEAD_DIM_A = 128
HEADS_A = 16
CH_A = 2048
POOL_WINDOWS = (2, 4, 8, 16)
POOL_GROUPS = 4
CH_B = 2048
POOL_GC = 512
POOL_BUF = 15
N_EXPERTS = 256
TOP_K = 8
N_GROUPS = 8
TOPK_GROUPS = 4
D_EXPERT = 512
D_SHARED = 512
ROUTE_SCALE = 2.5
PLE_DIM = 256
ALPHA = 2.0 ** 0.25
LN_EPS = 1e-5

T_PROMPT = BATCH * SEQ
T_SAMPLE = DEC_BATCH * DEC_SEQ
T_ALL = T_PROMPT + T_SAMPLE
N_ASSIGN = T_ALL * TOP_K

LANES = 128
SLAB_ROWS = D_MODEL // LANES
VMEM_LIMIT = 56 * 1024 * 1024

TM = 1088
TN = 512
TM_LN = 256
TM_SHARED = 256
TM_COMBINE = 64
POOL_TM = 512
POOL_HALO = 16
MOE_ROWS = 512
MOE_BUCKET = 128
MOE_SPLIT = 4
MOE_DQ = D_EXPERT // MOE_SPLIT
MOE_ITEMS = N_EXPERTS + N_ASSIGN // MOE_ROWS
MOE_SLAB_STRIDE = 40
MOE_NCHUNK = 1024


def _cparams(sem):
    return pltpu.CompilerParams(dimension_semantics=sem, vmem_limit_bytes=VMEM_LIMIT)


def _layer_norm(xf, g, b):
    mu = jnp.mean(xf, axis=-1, keepdims=True)
    xc = xf - mu
    var = jnp.mean(xc * xc, axis=-1, keepdims=True)
    return xc * lax.rsqrt(var + LN_EPS) * g + b


def _proj_kernel(x_ref, w_ref, o_ref, *, act):
    h = jnp.dot(x_ref[...], w_ref[...], preferred_element_type=F32)
    if act == "gelu":
        h = jax.nn.gelu(h, approximate=True)
    elif act == "sigmoid":
        h = jax.nn.sigmoid(h)
    o_ref[...] = h.astype(o_ref.dtype)


def _proj(x, w, col0, ncols, act, out_dtype):
    t, k = x.shape
    off = col0 // TN
    return pl.pallas_call(
        functools.partial(_proj_kernel, act=act),
        grid=(t // TM, ncols // TN),
        in_specs=[pl.BlockSpec((TM, k), lambda m, n: (m, 0)),
                  pl.BlockSpec((k, TN), lambda m, n: (0, n + off))],
        out_specs=pl.BlockSpec((TM, TN), lambda m, n: (m, n)),
        out_shape=jax.ShapeDtypeStruct((t, ncols), out_dtype),
        compiler_params=_cparams(("parallel", "arbitrary")),
        name="in_proj_" + act,
    )(x, w)


def _spatial_prompt_kernel(u_ref, gv_ref, ws_ref, bias_ref, g_ref, b_ref, o_ref):
    vn = _layer_norm(gv_ref[...].astype(F32), g_ref[...], b_ref[...])
    row = lax.broadcasted_iota(jnp.int32, (CHUNK, CHUNK), 0)
    col = lax.broadcasted_iota(jnp.int32, (CHUNK, CHUNK), 1)
    causal = col <= row
    for h in range(HEADS_A):
        sl = slice(h * HEAD_DIM_A, (h + 1) * HEAD_DIM_A)
        w = jnp.where(causal, ws_ref[h], 0.0).astype(BF16)
        s = jnp.dot(w, vn[:, sl].astype(BF16), preferred_element_type=F32) + bias_ref[:, sl]
        o_ref[:, sl] = (u_ref[:, sl].astype(F32) * s).astype(o_ref.dtype)


def _spatial_prompt(ug, w_s, bias_slab, ln_g, ln_b):
    n_chunks = T_PROMPT // CHUNK
    return pl.pallas_call(
        _spatial_prompt_kernel,
        grid=(n_chunks,),
        in_specs=[pl.BlockSpec((CHUNK, CH_A), lambda c: (c, 0)),
                  pl.BlockSpec((CHUNK, CH_A), lambda c: (c, 1)),
                  pl.BlockSpec((HEADS_A, CHUNK, CHUNK), lambda c: (0, 0, 0)),
                  pl.BlockSpec((CHUNK, CH_A), lambda c: (0, 0)),
                  pl.BlockSpec((1, CH_A), lambda c: (0, 0)),
                  pl.BlockSpec((1, CH_A), lambda c: (0, 0))],
        out_specs=pl.BlockSpec((CHUNK, CH_A), lambda c: (c, 0)),
        out_shape=jax.ShapeDtypeStruct((T_PROMPT, CH_A), BF16),
        compiler_params=_cparams(("parallel",)),
        name="spatial_prompt",
    )(ug, ug, w_s, bias_slab, ln_g, ln_b)


def _spatial_sample_kernel(u_ref, gv_ref, wexp_ref, bias_ref, g_ref, b_ref, o_ref, vn_ref):
    vn_ref[...] = _layer_norm(gv_ref[...].astype(F32), g_ref[...], b_ref[...])
    for t in range(DEC_SEQ):
        rows = slice(t * DEC_BATCH, (t + 1) * DEC_BATCH)
        s = jnp.broadcast_to(bias_ref[t:t + 1, :], (DEC_BATCH, CH_A))
        for j in range(t + 1):
            r = t * DEC_SEQ + j
            s = s + wexp_ref[r:r + 1, :] * vn_ref[j * DEC_BATCH:(j + 1) * DEC_BATCH, :]
        o_ref[rows, :] = (u_ref[rows, :].astype(F32) * s).astype(o_ref.dtype)


def _spatial_sample(ug, wexp, bias_slab, ln_g, ln_b):
    blk = T_PROMPT // T_SAMPLE
    return pl.pallas_call(
        _spatial_sample_kernel,
        grid=(1,),
        in_specs=[pl.BlockSpec((T_SAMPLE, CH_A), lambda i: (blk, 0)),
                  pl.BlockSpec((T_SAMPLE, CH_A), lambda i: (blk, 1)),
                  pl.BlockSpec((DEC_SEQ * DEC_SEQ, CH_A), lambda i: (0, 0)),
                  pl.BlockSpec((CHUNK, CH_A), lambda i: (0, 0)),
                  pl.BlockSpec((1, CH_A), lambda i: (0, 0)),
                  pl.BlockSpec((1, CH_A), lambda i: (0, 0))],
        out_specs=[pl.BlockSpec((T_SAMPLE, CH_A), lambda i: (0, 0)),
                   pl.BlockSpec((T_SAMPLE, CH_A), lambda i: (0, 0))],
        out_shape=[jax.ShapeDtypeStruct((T_SAMPLE, CH_A), BF16),
                   jax.ShapeDtypeStruct((T_SAMPLE, CH_A), F32)],
        compiler_params=_cparams(("arbitrary",)),
        name="spatial_sample",
    )(ug, ug, wexp, bias_slab, ln_g, ln_b)


def _pool_project(d_of_group, wp_ref, scale_ref, o_ref):
    for g in range(POOL_GROUPS):
        sl = slice(g * POOL_GC, (g + 1) * POOL_GC)
        y = jnp.dot(d_of_group(g).astype(BF16), wp_ref[g], preferred_element_type=F32)
        o_ref[:, sl] = (y * scale_ref[:, sl]).astype(o_ref.dtype)


def _pool_prompt_kernel(z_ref, halo_ref, wp_ref, scale_ref, o_ref, cat_ref):
    i = pl.program_id(1)
    halo = halo_ref[...]
    cat_ref[0:POOL_HALO, :] = jnp.where(i == 0, jnp.zeros_like(halo), halo)
    cat_ref[POOL_HALO:, :] = z_ref[...]
    pos = i * POOL_TM + lax.broadcasted_iota(jnp.int32, (POOL_TM, POOL_GC), 0)

    def d_of_group(g):
        win = POOL_WINDOWS[g]
        sl = slice(g * POOL_GC, (g + 1) * POOL_GC)
        acc = cat_ref[POOL_HALO:POOL_HALO + POOL_TM, sl]
        for k in range(1, win):
            acc = acc + cat_ref[POOL_HALO - k:POOL_HALO - k + POOL_TM, sl]
        cnt = jnp.minimum(pos + 1, win).astype(F32)
        return acc / cnt - z_ref[:, sl]

    _pool_project(d_of_group, wp_ref, scale_ref, o_ref)


def _pool_prompt(z, w_pool, pool_scale):
    tiles = SEQ // POOL_TM
    per_tile = POOL_TM // POOL_HALO
    return pl.pallas_call(
        _pool_prompt_kernel,
        grid=(BATCH, tiles),
        in_specs=[pl.BlockSpec((POOL_TM, CH_B), lambda b, i: (b * tiles + i, 0)),
                  pl.BlockSpec((POOL_HALO, CH_B),
                               lambda b, i: (jnp.maximum((b * tiles + i) * per_tile - 1, 0), 0)),
                  pl.BlockSpec((POOL_GROUPS, POOL_GC, POOL_GC), lambda b, i: (0, 0, 0)),
                  pl.BlockSpec((1, CH_B), lambda b, i: (0, 0))],
        out_specs=pl.BlockSpec((POOL_TM, CH_B), lambda b, i: (b * tiles + i, 0)),
        out_shape=jax.ShapeDtypeStruct((T_PROMPT, CH_B), BF16),
        scratch_shapes=[pltpu.VMEM((POOL_HALO + POOL_TM, CH_B), F32)],
        compiler_params=_cparams(("parallel", "arbitrary")),
        name="pool_prompt",
    )(z, z, w_pool, pool_scale)


def _pool_sample_kernel(z_ref, buf_ref, wp_ref, scale_ref, o_ref, d_ref):
    def cat_row(r, sl):
        if r < POOL_BUF:
            return buf_ref[r, :, sl]
        t = r - POOL_BUF
        return z_ref[t * DEC_BATCH:(t + 1) * DEC_BATCH, sl]

    for g in range(POOL_GROUPS):
        win = POOL_WINDOWS[g]
        sl = slice(g * POOL_GC, (g + 1) * POOL_GC)
        for t in range(DEC_SEQ):
            acc = cat_row(POOL_BUF + t, sl)
            for k in range(1, win):
                acc = acc + cat_row(POOL_BUF + t - k, sl)
            cnt = float(min(PAST_LEN + t + 1, win))
            d_ref[t * DEC_BATCH:(t + 1) * DEC_BATCH, sl] = acc / cnt - cat_row(POOL_BUF + t, sl)

    _pool_project(lambda g: d_ref[:, g * POOL_GC:(g + 1) * POOL_GC], wp_ref, scale_ref, o_ref)


def _pool_sample(z, buf_t, w_pool, pool_scale):
    blk = T_PROMPT // T_SAMPLE
    return pl.pallas_call(
        _pool_sample_kernel,
        grid=(1,),
        in_specs=[pl.BlockSpec((T_SAMPLE, CH_B), lambda i: (blk, 0)),
                  pl.BlockSpec((POOL_BUF, DEC_BATCH, CH_B), lambda i: (0, 0, 0)),
                  pl.BlockSpec((POOL_GROUPS, POOL_GC, POOL_GC), lambda i: (0, 0, 0)),
                  pl.BlockSpec((1, CH_B), lambda i: (0, 0))],
        out_specs=pl.BlockSpec((T_SAMPLE, CH_B), lambda i: (0, 0)),
        out_shape=jax.ShapeDtypeStruct((T_SAMPLE, CH_B), BF16),
        scratch_shapes=[pltpu.VMEM((T_SAMPLE, CH_B), F32)],
        compiler_params=_cparams(("arbitrary",)),
        name="pool_sample",
    )(z, buf_t, w_pool, pool_scale)


def _merge_kernel(oa_ref, ob_ref, wa_ref, wb_ref, ga_ref, gb_ref, o_ref):
    a = jnp.dot(oa_ref[...], wa_ref[...], preferred_element_type=F32)
    b = jnp.dot(ob_ref[...], wb_ref[...], preferred_element_type=F32)
    o_ref[...] = (ga_ref[...].astype(F32) * a + gb_ref[...].astype(F32) * b).astype(o_ref.dtype)


def _merge(o_a, o_b, w_up_a, w_up_b, gates):
    nb = D_MODEL // TN
    return pl.pallas_call(
        _merge_kernel,
        grid=(T_ALL // TM, nb),
        in_specs=[pl.BlockSpec((TM, CH_A), lambda m, n: (m, 0)),
                  pl.BlockSpec((TM, CH_B), lambda m, n: (m, 0)),
                  pl.BlockSpec((CH_A, TN), lambda m, n: (0, n)),
                  pl.BlockSpec((CH_B, TN), lambda m, n: (0, n)),
                  pl.BlockSpec((TM, TN), lambda m, n: (m, n)),
                  pl.BlockSpec((TM, TN), lambda m, n: (m, n + nb))],
        out_specs=pl.BlockSpec((TM, TN), lambda m, n: (m, n)),
        out_shape=jax.ShapeDtypeStruct((T_ALL, D_MODEL), BF16),
        compiler_params=_cparams(("parallel", "arbitrary")),
        name="merge",
    )(o_a, o_b, w_up_a, w_up_b, gates, gates)


def _out_proj_kernel(m_ref, w_ref, x_ref, o_ref):
    o_ref[...] = ALPHA * x_ref[...] + jnp.dot(m_ref[...], w_ref[...], preferred_element_type=F32)


def _out_proj(m, w_o, x):
    return pl.pallas_call(
        _out_proj_kernel,
        grid=(T_ALL // TM, D_MODEL // TN),
        in_specs=[pl.BlockSpec((TM, D_MODEL), lambda m_, n: (m_, 0)),
                  pl.BlockSpec((D_MODEL, TN), lambda m_, n: (0, n)),
                  pl.BlockSpec((TM, TN), lambda m_, n: (m_, n))],
        out_specs=pl.BlockSpec((TM, TN), lambda m_, n: (m_, n)),
        out_shape=jax.ShapeDtypeStruct((T_ALL, D_MODEL), F32),
        compiler_params=_cparams(("parallel", "arbitrary")),
        name="out_proj",
    )(m, w_o, x)


def _ln_router_kernel(r_ref, g_ref, b_ref, wr_ref, x_ref, xb_ref, s_ref):
    x1 = _layer_norm(r_ref[...], g_ref[...], b_ref[...])
    x_ref[...] = x1
    xb = x1.astype(BF16)
    xb_ref[...] = xb
    s_ref[...] = jax.nn.sigmoid(jnp.dot(xb, wr_ref[...], preferred_element_type=F32))


def _ln_router(r, g, b, w_router):
    row = lambda i: (i, 0)
    fixed = lambda i: (0, 0)
    return pl.pallas_call(
        _ln_router_kernel,
        grid=(T_ALL // TM_LN,),
        in_specs=[pl.BlockSpec((TM_LN, D_MODEL), row),
                  pl.BlockSpec((1, D_MODEL), fixed),
                  pl.BlockSpec((1, D_MODEL), fixed),
                  pl.BlockSpec((D_MODEL, N_EXPERTS), fixed)],
        out_specs=[pl.BlockSpec((TM_LN, D_MODEL), row),
                   pl.BlockSpec((TM_LN, D_MODEL), row),
                   pl.BlockSpec((TM_LN, N_EXPERTS), row)],
        out_shape=[jax.ShapeDtypeStruct((T_ALL, D_MODEL), F32),
                   jax.ShapeDtypeStruct((T_ALL, D_MODEL), BF16),
                   jax.ShapeDtypeStruct((T_ALL, N_EXPERTS), F32)],
        compiler_params=_cparams(("parallel",)),
        name="ln_router",
    )(r, g, b, w_router)


def _ln_kernel(r_ref, g_ref, b_ref, o_ref):
    o_ref[...] = _layer_norm(r_ref[...], g_ref[...], b_ref[...])


def _ln(r, g, b):
    row = lambda i: (i, 0)
    fixed = lambda i: (0, 0)
    return pl.pallas_call(
        _ln_kernel,
        grid=(T_ALL // TM_LN,),
        in_specs=[pl.BlockSpec((TM_LN, D_MODEL), row),
                  pl.BlockSpec((1, D_MODEL), fixed),
                  pl.BlockSpec((1, D_MODEL), fixed)],
        out_specs=pl.BlockSpec((TM_LN, D_MODEL), row),
        out_shape=jax.ShapeDtypeStruct((T_ALL, D_MODEL), F32),
        compiler_params=_cparams(("parallel",)),
        name="ln_final",
    )(r, g, b)


def _moe_kernel(item_e, item_start, item_n, sdst,
                x_hbm, w1_ref, w3_ref, w2_ref, y_hbm,
                xslab, xb, w13b, w2b, yslab, gsem, ssem):
    del item_e
    w = pl.program_id(0)
    q = pl.program_id(1)
    n = item_n[w]
    start = item_start[w]

    @pl.when(jnp.logical_and(w == 0, q == 0))
    def _init():
        xslab[...] = jnp.zeros_like(xslab)
        yslab[...] = jnp.zeros_like(yslab)

    def x_copy(tok, r):
        return pltpu.make_async_copy(x_hbm.at[tok], xslab.at[pl.ds(r * MOE_SLAB_STRIDE, SLAB_ROWS), :], gsem)

    def y_copy(r, dst):
        return pltpu.make_async_copy(yslab.at[pl.ds(r * MOE_SLAB_STRIDE, SLAB_ROWS), :], y_hbm.at[dst], ssem)

    @pl.when(jnp.logical_and(q == 0, n > 0))
    def _gather():
        def issue(r, c):
            x_copy(sdst[start + r] // TOP_K, r).start()
            return c

        def wait(r, c):
            x_copy(0, r).wait()
            return c

        lax.fori_loop(0, n, issue, 0)
        lax.fori_loop(0, n, wait, 0)
        for s in range(SLAB_ROWS):
            xb[:, s * LANES:(s + 1) * LANES] = xslab[pl.ds(s, MOE_ROWS, stride=MOE_SLAB_STRIDE), :].astype(BF16)

    @pl.when(n > 0)
    def _compute():
        w13b[:, :MOE_DQ] = w1_ref[...].astype(BF16)
        w13b[:, MOE_DQ:] = w3_ref[...].astype(BF16)
        w2b[...] = w2_ref[...].astype(BF16)
        nb = (n + MOE_BUCKET - 1) // MOE_BUCKET
        for b in range(1, MOE_ROWS // MOE_BUCKET + 1):
            rows = b * MOE_BUCKET

            @pl.when(nb == b)
            def _bucket(rows=rows):
                h = jnp.dot(xb[0:rows, :], w13b[...], preferred_element_type=F32)
                a = (jax.nn.silu(h[:, :MOE_DQ]) * h[:, MOE_DQ:]).astype(BF16)
                for c in range(D_MODEL // MOE_NCHUNK):
                    yq = jnp.dot(a, w2b[:, c * MOE_NCHUNK:(c + 1) * MOE_NCHUNK], preferred_element_type=F32)
                    for s in range(MOE_NCHUNK // LANES):
                        idx = pl.ds(c * (MOE_NCHUNK // LANES) + s, rows, stride=MOE_SLAB_STRIDE)
                        part = yq[:, s * LANES:(s + 1) * LANES]
                        yslab[idx, :] = jnp.where(q == 0, part, yslab[idx, :] + part)

    @pl.when(jnp.logical_and(q == MOE_SPLIT - 1, n > 0))
    def _scatter():
        def issue(r, c):
            y_copy(r, sdst[start + r]).start()
            return c

        def wait(r, c):
            y_copy(r, 0).wait()
            return c

        lax.fori_loop(0, n, issue, 0)
        lax.fori_loop(0, n, wait, 0)


def _moe_routed(x_slab, w1, w3, w2, item_e, item_start, item_n, sdst):
    def q_eff(w, q, item_n):
        return jnp.where(item_n[w] > 0, q, MOE_SPLIT - 1)

    grid_spec = pltpu.PrefetchScalarGridSpec(
        num_scalar_prefetch=4,
        grid=(MOE_ITEMS, MOE_SPLIT),
        in_specs=[
            pl.BlockSpec(memory_space=pl.ANY),
            pl.BlockSpec((None, D_MODEL, MOE_DQ), lambda w, q, ie, ist, inn, sd: (ie[w], 0, q_eff(w, q, inn))),
            pl.BlockSpec((None, D_MODEL, MOE_DQ), lambda w, q, ie, ist, inn, sd: (ie[w], 0, q_eff(w, q, inn))),
            pl.BlockSpec((None, MOE_DQ, D_MODEL), lambda w, q, ie, ist, inn, sd: (ie[w], q_eff(w, q, inn), 0)),
        ],
        out_specs=pl.BlockSpec(memory_space=pl.ANY),
        scratch_shapes=[
            pltpu.VMEM((MOE_ROWS * MOE_SLAB_STRIDE, LANES), F32),
            pltpu.VMEM((MOE_ROWS, D_MODEL), BF16),
            pltpu.VMEM((D_MODEL, 2 * MOE_DQ), BF16),
            pltpu.VMEM((MOE_DQ, D_MODEL), BF16),
            pltpu.VMEM((MOE_ROWS * MOE_SLAB_STRIDE, LANES), F32),
            pltpu.SemaphoreType.DMA(()),
            pltpu.SemaphoreType.DMA(()),
        ],
    )
    return pl.pallas_call(
        _moe_kernel,
        grid_spec=grid_spec,
        out_shape=jax.ShapeDtypeStruct((N_ASSIGN, SLAB_ROWS, LANES), F32),
        compiler_params=_cparams(("arbitrary", "arbitrary")),
        name="moe_routed",
    )(item_e, item_start, item_n, sdst, x_slab, w1, w3, w2)


def _shared_kernel(x_ref, w1_ref, w3_ref, w2_ref, o_ref):
    x = x_ref[...]
    a = jax.nn.silu(jnp.dot(x, w1_ref[...], preferred_element_type=F32))
    a = (a * jnp.dot(x, w3_ref[...], preferred_element_type=F32)).astype(BF16)
    o_ref[...] = jnp.dot(a, w2_ref[...], preferred_element_type=F32).astype(o_ref.dtype)


def _shared_expert(xb, ws1, ws3, ws2):
    fixed = lambda i: (0, 0)
    return pl.pallas_call(
        _shared_kernel,
        grid=(T_ALL // TM_SHARED,),
        in_specs=[pl.BlockSpec((TM_SHARED, D_MODEL), lambda i: (i, 0)),
                  pl.BlockSpec((D_MODEL, D_SHARED), fixed),
                  pl.BlockSpec((D_MODEL, D_SHARED), fixed),
                  pl.BlockSpec((D_SHARED, D_MODEL), fixed)],
        out_specs=pl.BlockSpec((TM_SHARED, D_MODEL), lambda i: (i, 0)),
        out_shape=jax.ShapeDtypeStruct((T_ALL, D_MODEL), F32),
        compiler_params=_cparams(("parallel",)),
        name="shared_expert",
    )(xb, ws1, ws3, ws2)


def _combine_kernel(y_ref, gw_ref, x_ref, sh_ref, g_ref, b_ref, o_ref):
    acc = ALPHA * x_ref[...] + sh_ref[...]
    for k in range(TOP_K):
        acc = acc + y_ref[:, k] * gw_ref[:, k:k + 1, :]
    inv_d = 1.0 / D_MODEL
    mu = jnp.sum(jnp.sum(acc, axis=2, keepdims=True), axis=1, keepdims=True) * inv_d
    xc = acc - mu
    var = jnp.sum(jnp.sum(xc * xc, axis=2, keepdims=True), axis=1, keepdims=True) * inv_d
    o_ref[...] = xc * lax.rsqrt(var + LN_EPS) * g_ref[...] + b_ref[...]


def _combine(y, gwb, x_slab, sh_slab, g_slab, b_slab):
    tok = lambda i: (i, 0, 0)
    return pl.pallas_call(
        _combine_kernel,
        grid=(T_ALL // TM_COMBINE,),
        in_specs=[pl.BlockSpec((TM_COMBINE, TOP_K, SLAB_ROWS, LANES), lambda i: (i, 0, 0, 0)),
                  pl.BlockSpec((TM_COMBINE, TOP_K, LANES), tok),
                  pl.BlockSpec((TM_COMBINE, SLAB_ROWS, LANES), tok),
                  pl.BlockSpec((TM_COMBINE, SLAB_ROWS, LANES), tok),
                  pl.BlockSpec((1, SLAB_ROWS, LANES), lambda i: (0, 0, 0)),
                  pl.BlockSpec((1, SLAB_ROWS, LANES), lambda i: (0, 0, 0))],
        out_specs=pl.BlockSpec((TM_COMBINE, SLAB_ROWS, LANES), tok),
        out_shape=jax.ShapeDtypeStruct((T_ALL, SLAB_ROWS, LANES), F32),
        compiler_params=_cparams(("parallel",)),
        name="moe_combine_ln",
    )(y, gwb, x_slab, sh_slab, g_slab, b_slab)


def _ple_kernel(xb_ref, wg_ref, p_ref, wp_ref, x_ref, o_ref):
    gate = jax.nn.sigmoid(jnp.dot(xb_ref[...], wg_ref[...], preferred_element_type=F32))
    proj = jnp.dot(p_ref[...], wp_ref[...], preferred_element_type=F32)
    o_ref[...] = ALPHA * x_ref[...] + gate * proj


def _ple(xb, w_gate, p, w_proj, x):
    return pl.pallas_call(
        _ple_kernel,
        grid=(T_ALL // TM, D_MODEL // TN),
        in_specs=[pl.BlockSpec((TM, D_MODEL), lambda m, n: (m, 0)),
                  pl.BlockSpec((D_MODEL, TN), lambda m, n: (0, n)),
                  pl.BlockSpec((TM, PLE_DIM), lambda m, n: (m, 0)),
                  pl.BlockSpec((PLE_DIM, TN), lambda m, n: (0, n)),
                  pl.BlockSpec((TM, TN), lambda m, n: (m, n))],
        out_specs=pl.BlockSpec((TM, TN), lambda m, n: (m, n)),
        out_shape=jax.ShapeDtypeStruct((T_ALL, D_MODEL), F32),
        compiler_params=_cparams(("parallel", "arbitrary")),
        name="ple",
    )(xb, w_gate, p, w_proj, x)


def _route(scores, b_router):
    t = scores.shape[0]
    biased = scores + b_router.astype(F32)
    per_group = N_EXPERTS // N_GROUPS
    gscore = lax.top_k(biased.reshape(t, N_GROUPS, per_group), 2)[0].sum(-1)
    _, gidx = lax.top_k(gscore, TOPK_GROUPS)
    gmask = jax.nn.one_hot(gidx, N_GROUPS).sum(1) > 0
    masked = jnp.where(jnp.repeat(gmask, per_group, axis=1), biased, -jnp.inf)
    _, eidx = lax.top_k(masked, TOP_K)
    gw = jnp.take_along_axis(scores, eidx, axis=1)
    gw = gw / jnp.sum(gw, axis=-1, keepdims=True) * ROUTE_SCALE
    return eidx.astype(jnp.int32), gw


def _dispatch(eidx):
    flat_e = eidx.reshape(-1)
    sdst = jnp.argsort(flat_e).astype(jnp.int32)
    counts = jnp.bincount(flat_e, length=N_EXPERTS).astype(jnp.int32)
    sstart = jnp.cumsum(counts) - counts
    n_items = (counts + MOE_ROWS - 1) // MOE_ROWS
    item_end = jnp.cumsum(n_items)
    total = item_end[-1]
    w = jnp.arange(MOE_ITEMS, dtype=jnp.int32)
    e_of = jnp.minimum(jnp.searchsorted(item_end, w, side="right"), N_EXPERTS - 1).astype(jnp.int32)
    local = w - (item_end - n_items)[e_of]
    valid = w < total
    e_last = e_of[jnp.maximum(total - 1, 0)]
    item_e = jnp.where(valid, e_of, e_last)
    item_start = jnp.where(valid, sstart[e_of] + local * MOE_ROWS, 0)
    item_n = jnp.where(valid, jnp.clip(counts[e_of] - local * MOE_ROWS, 0, MOE_ROWS), 0)
    return item_e.astype(jnp.int32), item_start.astype(jnp.int32), item_n.astype(jnp.int32), sdst


def kernel(x_prompt, x_sample, state_pool, p_prompt, p_sample, w_in, ln_v_g, ln_v_b, w_s, b_s, w_pool, pool_scale, w_up_a, w_up_b, w_o, ln_g, ln_b, w_router, b_router, w1, w3, w2, ws1, ws3, ws2, w_ple_gate, w_ple_proj):
    def tokens(a_prompt, a_sample):
        d = a_prompt.shape[-1]
        return jnp.concatenate([a_prompt.reshape(T_PROMPT, d),
                                a_sample.transpose(1, 0, 2).reshape(T_SAMPLE, d)], axis=0)

    x = tokens(x_prompt, x_sample)
    xb = x.astype(BF16)
    pb = tokens(p_prompt[0], p_sample[0]).astype(BF16)
    row2d = lambda v: v.reshape(1, -1).astype(F32)

    w_in_b = w_in[0].astype(BF16)
    ug = _proj(xb, w_in_b, 0, 2 * CH_A, "gelu", BF16)
    z = _proj(xb, w_in_b, 2 * CH_A, CH_B, "none", F32)
    gates = _proj(xb, w_in_b, 2 * CH_A + CH_B, 2 * D_MODEL, "sigmoid", BF16)

    bias_slab = jnp.repeat(b_s[0].T, HEAD_DIM_A, axis=1).astype(F32)
    wexp = jnp.repeat(w_s[0][:, :DEC_SEQ, :DEC_SEQ].transpose(1, 2, 0).reshape(DEC_SEQ * DEC_SEQ, HEADS_A),
                      HEAD_DIM_A, axis=1).astype(F32)
    lvg, lvb = row2d(ln_v_g[0]), row2d(ln_v_b[0])
    oa_p = _spatial_prompt(ug, w_s[0], bias_slab, lvg, lvb)
    oa_s, vn_s = _spatial_sample(ug, wexp, bias_slab, lvg, lvb)
    o_a = jnp.concatenate([oa_p, oa_s], axis=0)

    w_pool_b = w_pool[0].astype(BF16)
    pscale = row2d(pool_scale[0])
    buf_t = state_pool[0].transpose(1, 0, 2)
    ob_p = _pool_prompt(z, w_pool_b, pscale)
    ob_s = _pool_sample(z, buf_t, w_pool_b, pscale)
    o_b = jnp.concatenate([ob_p, ob_s], axis=0)

    m = _merge(o_a, o_b, w_up_a[0].astype(BF16), w_up_b[0].astype(BF16), gates)
    r1 = _out_proj(m, w_o[0].astype(BF16), x)
    x1, x1b, scores = _ln_router(r1, row2d(ln_g[0, 0]), row2d(ln_b[0, 0]), w_router[0].astype(BF16))

    eidx, gw = _route(scores, b_router[0])
    item_e, item_start, item_n, sdst = _dispatch(eidx)
    x1_slab = x1.reshape(T_ALL, SLAB_ROWS, LANES)
    y = _moe_routed(x1_slab, w1[0], w3[0], w2[0], item_e, item_start, item_n, sdst)
    sh = _shared_expert(x1b, ws1[0].astype(BF16), ws3[0].astype(BF16), ws2[0].astype(BF16))
    gwb = jnp.broadcast_to(gw.astype(F32).reshape(T_ALL, TOP_K, 1), (T_ALL, TOP_K, LANES))
    slab = lambda v: v.reshape(1, SLAB_ROWS, LANES).astype(F32)
    x2 = _combine(y.reshape(T_ALL, TOP_K, SLAB_ROWS, LANES), gwb, x1_slab,
                  sh.reshape(T_ALL, SLAB_ROWS, LANES), slab(ln_g[0, 1]), slab(ln_b[0, 1]))
    x2 = x2.reshape(T_ALL, D_MODEL)

    r3 = _ple(x2.astype(BF16), w_ple_gate[0].astype(BF16), pb, w_ple_proj[0].astype(BF16), x2)
    x3 = _ln(r3, row2d(ln_g[0, 2]), row2d(ln_b[0, 2]))

    y_prompt = x3[:T_PROMPT].reshape(BATCH, SEQ, D_MODEL)
    y_sample = x3[T_PROMPT:].reshape(DEC_SEQ, DEC_BATCH, D_MODEL).transpose(1, 0, 2)
    z_p = z[:T_PROMPT].reshape(BATCH, SEQ, CH_B)
    z_s = z[T_PROMPT:].reshape(DEC_SEQ, DEC_BATCH, CH_B).transpose(1, 0, 2)
    new_pool_prompt = z_p[:, SEQ - POOL_BUF:][None]
    new_pool_sample = jnp.concatenate([state_pool[0][:, DEC_SEQ:], z_s], axis=1)[None]
    new_chunk_v_sample = vn_s.reshape(DEC_SEQ, DEC_BATCH, CH_A).transpose(1, 0, 2)[None]
    return (y_prompt, y_sample, new_pool_prompt, new_pool_sample, new_chunk_v_sample)
```

```python
import functools

import jax
import jax.numpy as jnp
from jax import lax
from jax.experimental import pallas as pl
from jax.experimental.pallas import tpu as pltpu

F32 = jnp.float32
BF16 = jnp.bfloat16

D_MODEL = 4096
BATCH = 4
SEQ = 2048
DEC_BATCH = 128
DEC_SEQ = 4
PAST_LEN = 16384
CHUNK = 128
HEAD_DIM_A = 128
HEADS_A = 16
CH_A = 2048
POOL_WINDOWS = (2, 4, 8, 16)
POOL_GROUPS = 4
CH_B = 2048
POOL_GC = 512
POOL_BUF = 15
N_EXPERTS = 256
TOP_K = 8
N_GROUPS = 8
TOPK_GROUPS = 4
D_EXPERT = 512
D_SHARED = 512
ROUTE_SCALE = 2.5
PLE_DIM = 256
ALPHA = 2.0 ** 0.25
LN_EPS = 1e-5

T_PROMPT = BATCH * SEQ
T_SAMPLE = DEC_BATCH * DEC_SEQ
T_ALL = T_PROMPT + T_SAMPLE
N_ASSIGN = T_ALL * TOP_K

LANES = 128
SLAB_ROWS = D_MODEL // LANES
VMEM_LIMIT = 56 * 1024 * 1024

TM = 1088
TN = 512
TM_LN = 256
TM_SHARED = 256
TM_COMBINE = 64
POOL_TM = 512
POOL_HALO = 16
MOE_ROWS = 512
MOE_BUCKET = 64
MOE_SPLIT = 4
MOE_DQ = D_EXPERT // MOE_SPLIT
MOE_ITEMS = N_EXPERTS + N_ASSIGN // MOE_ROWS
HALF = D_MODEL // 2
PK_ROWS = HALF // LANES
PK_STRIDE = 24
MOE_NCHUNK = 512
MOE_DMA_UNROLL = 8
TOP_K_SHIFT = 3
HI_MASK = 0xFFFF0000


def _pack_pair(hi, lo):
    h = lax.bitcast_convert_type(hi.astype(BF16).astype(F32), jnp.uint32)
    l = lax.bitcast_convert_type(lo.astype(BF16).astype(F32), jnp.uint32)
    return h | (l >> jnp.uint32(16))


def _unpack_pair(w):
    hi = lax.bitcast_convert_type(w & jnp.uint32(HI_MASK), F32)
    lo = lax.bitcast_convert_type(w << jnp.uint32(16), F32)
    return hi, lo


def _cparams(sem):
    return pltpu.CompilerParams(dimension_semantics=sem, vmem_limit_bytes=VMEM_LIMIT)


def _layer_norm(xf, g, b):
    mu = jnp.mean(xf, axis=-1, keepdims=True)
    xc = xf - mu
    var = jnp.mean(xc * xc, axis=-1, keepdims=True)
    return xc * lax.rsqrt(var + LN_EPS) * g + b


def _proj_kernel(x_ref, w_ref, o_ref, *, act):
    h = jnp.dot(x_ref[...], w_ref[...], preferred_element_type=F32)
    if act == "gelu":
        h = jax.nn.gelu(h, approximate=True)
    elif act == "sigmoid":
        h = jax.nn.sigmoid(h)
    o_ref[...] = h.astype(o_ref.dtype)


def _proj(x, w, col0, ncols, act, out_dtype):
    t, k = x.shape
    off = col0 // TN
    return pl.pallas_call(
        functools.partial(_proj_kernel, act=act),
        grid=(t // TM, ncols // TN),
        in_specs=[pl.BlockSpec((TM, k), lambda m, n: (m, 0)),
                  pl.BlockSpec((k, TN), lambda m, n: (0, n + off))],
        out_specs=pl.BlockSpec((TM, TN), lambda m, n: (m, n)),
        out_shape=jax.ShapeDtypeStruct((t, ncols), out_dtype),
        compiler_params=_cparams(("parallel", "arbitrary")),
        name="in_proj_" + act,
    )(x, w)


def _spatial_prompt_kernel(u_ref, gv_ref, ws_ref, bias_ref, g_ref, b_ref, o_ref):
    vn = _layer_norm(gv_ref[...].astype(F32), g_ref[...], b_ref[...])
    row = lax.broadcasted_iota(jnp.int32, (CHUNK, CHUNK), 0)
    col = lax.broadcasted_iota(jnp.int32, (CHUNK, CHUNK), 1)
    causal = col <= row
    for h in range(HEADS_A):
        sl = slice(h * HEAD_DIM_A, (h + 1) * HEAD_DIM_A)
        w = jnp.where(causal, ws_ref[h], 0.0).astype(BF16)
        s = jnp.dot(w, vn[:, sl].astype(BF16), preferred_element_type=F32) + bias_ref[:, sl]
        o_ref[:, sl] = (u_ref[:, sl].astype(F32) * s).astype(o_ref.dtype)


def _spatial_prompt(ug, w_s, bias_slab, ln_g, ln_b):
    n_chunks = T_PROMPT // CHUNK
    return pl.pallas_call(
        _spatial_prompt_kernel,
        grid=(n_chunks,),
        in_specs=[pl.BlockSpec((CHUNK, CH_A), lambda c: (c, 0)),
                  pl.BlockSpec((CHUNK, CH_A), lambda c: (c, 1)),
                  pl.BlockSpec((HEADS_A, CHUNK, CHUNK), lambda c: (0, 0, 0)),
                  pl.BlockSpec((CHUNK, CH_A), lambda c: (0, 0)),
                  pl.BlockSpec((1, CH_A), lambda c: (0, 0)),
                  pl.BlockSpec((1, CH_A), lambda c: (0, 0))],
        out_specs=pl.BlockSpec((CHUNK, CH_A), lambda c: (c, 0)),
        out_shape=jax.ShapeDtypeStruct((T_PROMPT, CH_A), BF16),
        compiler_params=_cparams(("parallel",)),
        name="spatial_prompt",
    )(ug, ug, w_s, bias_slab, ln_g, ln_b)


def _spatial_sample_kernel(u_ref, gv_ref, wexp_ref, bias_ref, g_ref, b_ref, o_ref, vn_ref):
    vn_ref[...] = _layer_norm(gv_ref[...].astype(F32), g_ref[...], b_ref[...])
    for t in range(DEC_SEQ):
        rows = slice(t * DEC_BATCH, (t + 1) * DEC_BATCH)
        s = jnp.broadcast_to(bias_ref[t:t + 1, :], (DEC_BATCH, CH_A))
        for j in range(t + 1):
            r = t * DEC_SEQ + j
            s = s + wexp_ref[r:r + 1, :] * vn_ref[j * DEC_BATCH:(j + 1) * DEC_BATCH, :]
        o_ref[rows, :] = (u_ref[rows, :].astype(F32) * s).astype(o_ref.dtype)


def _spatial_sample(ug, wexp, bias_slab, ln_g, ln_b):
    blk = T_PROMPT // T_SAMPLE
    return pl.pallas_call(
        _spatial_sample_kernel,
        grid=(1,),
        in_specs=[pl.BlockSpec((T_SAMPLE, CH_A), lambda i: (blk, 0)),
                  pl.BlockSpec((T_SAMPLE, CH_A), lambda i: (blk, 1)),
                  pl.BlockSpec((DEC_SEQ * DEC_SEQ, CH_A), lambda i: (0, 0)),
                  pl.BlockSpec((CHUNK, CH_A), lambda i: (0, 0)),
                  pl.BlockSpec((1, CH_A), lambda i: (0, 0)),
                  pl.BlockSpec((1, CH_A), lambda i: (0, 0))],
        out_specs=[pl.BlockSpec((T_SAMPLE, CH_A), lambda i: (0, 0)),
                   pl.BlockSpec((T_SAMPLE, CH_A), lambda i: (0, 0))],
        out_shape=[jax.ShapeDtypeStruct((T_SAMPLE, CH_A), BF16),
                   jax.ShapeDtypeStruct((T_SAMPLE, CH_A), F32)],
        compiler_params=_cparams(("arbitrary",)),
        name="spatial_sample",
    )(ug, ug, wexp, bias_slab, ln_g, ln_b)


def _pool_project(d_of_group, wp_ref, scale_ref, o_ref):
    for g in range(POOL_GROUPS):
        sl = slice(g * POOL_GC, (g + 1) * POOL_GC)
        y = jnp.dot(d_of_group(g).astype(BF16), wp_ref[g], preferred_element_type=F32)
        o_ref[:, sl] = (y * scale_ref[:, sl]).astype(o_ref.dtype)


def _pool_prompt_kernel(z_ref, halo_ref, wp_ref, scale_ref, o_ref, cat_ref):
    i = pl.program_id(1)
    halo = halo_ref[...]
    cat_ref[0:POOL_HALO, :] = jnp.where(i == 0, jnp.zeros_like(halo), halo)
    cat_ref[POOL_HALO:, :] = z_ref[...]
    pos = i * POOL_TM + lax.broadcasted_iota(jnp.int32, (POOL_TM, POOL_GC), 0)

    def d_of_group(g):
        win = POOL_WINDOWS[g]
        sl = slice(g * POOL_GC, (g + 1) * POOL_GC)
        acc = cat_ref[POOL_HALO:POOL_HALO + POOL_TM, sl]
        for k in range(1, win):
            acc = acc + cat_ref[POOL_HALO - k:POOL_HALO - k + POOL_TM, sl]
        cnt = jnp.minimum(pos + 1, win).astype(F32)
        return acc / cnt - z_ref[:, sl]

    _pool_project(d_of_group, wp_ref, scale_ref, o_ref)


def _pool_prompt(z, w_pool, pool_scale):
    tiles = SEQ // POOL_TM
    per_tile = POOL_TM // POOL_HALO
    return pl.pallas_call(
        _pool_prompt_kernel,
        grid=(BATCH, tiles),
        in_specs=[pl.BlockSpec((POOL_TM, CH_B), lambda b, i: (b * tiles + i, 0)),
                  pl.BlockSpec((POOL_HALO, CH_B),
                               lambda b, i: (jnp.maximum((b * tiles + i) * per_tile - 1, 0), 0)),
                  pl.BlockSpec((POOL_GROUPS, POOL_GC, POOL_GC), lambda b, i: (0, 0, 0)),
                  pl.BlockSpec((1, CH_B), lambda b, i: (0, 0))],
        out_specs=pl.BlockSpec((POOL_TM, CH_B), lambda b, i: (b * tiles + i, 0)),
        out_shape=jax.ShapeDtypeStruct((T_PROMPT, CH_B), BF16),
        scratch_shapes=[pltpu.VMEM((POOL_HALO + POOL_TM, CH_B), F32)],
        compiler_params=_cparams(("parallel", "arbitrary")),
        name="pool_prompt",
    )(z, z, w_pool, pool_scale)


def _pool_sample_kernel(z_ref, buf_ref, wp_ref, scale_ref, o_ref, d_ref):
    def cat_row(r, sl):
        if r < POOL_BUF:
            return buf_ref[r, :, sl]
        t = r - POOL_BUF
        return z_ref[t * DEC_BATCH:(t + 1) * DEC_BATCH, sl]

    for g in range(POOL_GROUPS):
        win = POOL_WINDOWS[g]
        sl = slice(g * POOL_GC, (g + 1) * POOL_GC)
        for t in range(DEC_SEQ):
            acc = cat_row(POOL_BUF + t, sl)
            for k in range(1, win):
                acc = acc + cat_row(POOL_BUF + t - k, sl)
            cnt = float(min(PAST_LEN + t + 1, win))
            d_ref[t * DEC_BATCH:(t + 1) * DEC_BATCH, sl] = acc / cnt - cat_row(POOL_BUF + t, sl)

    _pool_project(lambda g: d_ref[:, g * POOL_GC:(g + 1) * POOL_GC], wp_ref, scale_ref, o_ref)


def _pool_sample(z, buf_t, w_pool, pool_scale):
    blk = T_PROMPT // T_SAMPLE
    return pl.pallas_call(
        _pool_sample_kernel,
        grid=(1,),
        in_specs=[pl.BlockSpec((T_SAMPLE, CH_B), lambda i: (blk, 0)),
                  pl.BlockSpec((POOL_BUF, DEC_BATCH, CH_B), lambda i: (0, 0, 0)),
                  pl.BlockSpec((POOL_GROUPS, POOL_GC, POOL_GC), lambda i: (0, 0, 0)),
                  pl.BlockSpec((1, CH_B), lambda i: (0, 0))],
        out_specs=pl.BlockSpec((T_SAMPLE, CH_B), lambda i: (0, 0)),
        out_shape=jax.ShapeDtypeStruct((T_SAMPLE, CH_B), BF16),
        scratch_shapes=[pltpu.VMEM((T_SAMPLE, CH_B), F32)],
        compiler_params=_cparams(("arbitrary",)),
        name="pool_sample",
    )(z, buf_t, w_pool, pool_scale)


def _merge_kernel(oa_ref, ob_ref, wa_ref, wb_ref, ga_ref, gb_ref, o_ref):
    a = jnp.dot(oa_ref[...], wa_ref[...], preferred_element_type=F32)
    b = jnp.dot(ob_ref[...], wb_ref[...], preferred_element_type=F32)
    o_ref[...] = (ga_ref[...].astype(F32) * a + gb_ref[...].astype(F32) * b).astype(o_ref.dtype)


def _merge(o_a, o_b, w_up_a, w_up_b, gates):
    nb = D_MODEL // TN
    return pl.pallas_call(
        _merge_kernel,
        grid=(T_ALL // TM, nb),
        in_specs=[pl.BlockSpec((TM, CH_A), lambda m, n: (m, 0)),
                  pl.BlockSpec((TM, CH_B), lambda m, n: (m, 0)),
                  pl.BlockSpec((CH_A, TN), lambda m, n: (0, n)),
                  pl.BlockSpec((CH_B, TN), lambda m, n: (0, n)),
                  pl.BlockSpec((TM, TN), lambda m, n: (m, n)),
                  pl.BlockSpec((TM, TN), lambda m, n: (m, n + nb))],
        out_specs=pl.BlockSpec((TM, TN), lambda m, n: (m, n)),
        out_shape=jax.ShapeDtypeStruct((T_ALL, D_MODEL), BF16),
        compiler_params=_cparams(("parallel", "arbitrary")),
        name="merge",
    )(o_a, o_b, w_up_a, w_up_b, gates, gates)


def _out_proj_kernel(m_ref, w_ref, x_ref, o_ref):
    o_ref[...] = ALPHA * x_ref[...] + jnp.dot(m_ref[...], w_ref[...], preferred_element_type=F32)


def _out_proj(m, w_o, x):
    return pl.pallas_call(
        _out_proj_kernel,
        grid=(T_ALL // TM, D_MODEL // TN),
        in_specs=[pl.BlockSpec((TM, D_MODEL), lambda m_, n: (m_, 0)),
                  pl.BlockSpec((D_MODEL, TN), lambda m_, n: (0, n)),
                  pl.BlockSpec((TM, TN), lambda m_, n: (m_, n))],
        out_specs=pl.BlockSpec((TM, TN), lambda m_, n: (m_, n)),
        out_shape=jax.ShapeDtypeStruct((T_ALL, D_MODEL), F32),
        compiler_params=_cparams(("parallel", "arbitrary")),
        name="out_proj",
    )(m, w_o, x)


def _route_tile(scores, bias, carry):
    tm = scores.shape[0]
    lane = lax.broadcasted_iota(jnp.int32, (tm, N_EXPERTS), 1)
    per_group = N_EXPERTS // N_GROUPS
    neg = -jnp.inf
    biased = scores + bias

    def first_argmax(v):
        m = jnp.max(v, axis=-1, keepdims=True)
        return m, jnp.min(jnp.where(v == m, lane, N_EXPERTS), axis=-1, keepdims=True)

    in_group = [jnp.logical_and(lane >= g * per_group, lane < (g + 1) * per_group) for g in range(N_GROUPS)]
    gscore = []
    for g in range(N_GROUPS):
        v = jnp.where(in_group[g], biased, neg)
        m1, i1 = first_argmax(v)
        m2 = jnp.max(jnp.where(lane == i1, neg, v), axis=-1, keepdims=True)
        gscore.append(m1 + m2)
    group_kept = jnp.zeros((tm, N_EXPERTS), jnp.int32)
    for g in range(N_GROUPS):
        beaten_by = jnp.zeros((tm, 1), jnp.int32)
        for h in range(N_GROUPS):
            if h == g:
                continue
            wins = gscore[h] >= gscore[g] if h < g else gscore[h] > gscore[g]
            beaten_by = beaten_by + jnp.where(wins, 1, 0)
        kept = jnp.where(beaten_by < TOPK_GROUPS, 1, 0)
        group_kept = group_kept + jnp.where(in_group[g], kept, 0)
    masked = jnp.where(group_kept > 0, biased, neg)

    idx, gate = [], []
    sel_f = jnp.zeros((tm, N_EXPERTS), F32)
    for _ in range(TOP_K):
        _, i = first_argmax(masked)
        hit = lane == i
        idx.append(i)
        gate.append(jnp.sum(jnp.where(hit, scores, 0.0), axis=-1, keepdims=True))
        sel_f = sel_f + jnp.where(hit, 1.0, 0.0)
        masked = jnp.where(hit, neg, masked)
    total = gate[0]
    for k in range(1, TOP_K):
        total = total + gate[k]
    gate = [gk / total * ROUTE_SCALE for gk in gate]

    r_i = lax.broadcasted_iota(jnp.int32, (tm, tm), 0)
    c_i = lax.broadcasted_iota(jnp.int32, (tm, tm), 1)
    lower = jnp.where(c_i < r_i, 1.0, 0.0).astype(BF16)
    rank_full = carry + jnp.dot(lower, sel_f.astype(BF16), preferred_element_type=F32)
    rank = [jnp.sum(jnp.where(lane == i, rank_full, 0.0), axis=-1, keepdims=True) for i in idx]
    return idx, gate, rank, jnp.sum(sel_f, axis=0, keepdims=True)


def _columns_to_lanes(cols, dtype):
    tm = cols[0].shape[0]
    lane = lax.broadcasted_iota(jnp.int32, (tm, LANES), 1)
    out = jnp.zeros((tm, LANES), dtype)
    for k, c in enumerate(cols):
        out = jnp.where(lane == k, c.astype(dtype), out)
    return out


def _ln_router_kernel(r_ref, g_ref, b_ref, wr_ref, br_ref, x_ref, xb_ref, xpk_ref, e_ref, gw_ref, rk_ref, cnt_ref):
    @pl.when(pl.program_id(0) == 0)
    def _init():
        cnt_ref[...] = jnp.zeros_like(cnt_ref)

    x1 = _layer_norm(r_ref[...], g_ref[...], b_ref[...])
    x_ref[...] = x1
    xb = x1.astype(BF16)
    xb_ref[...] = xb
    xpk_ref[...] = _pack_pair(x1[:, :HALF], x1[:, HALF:])
    scores = jax.nn.sigmoid(jnp.dot(xb, wr_ref[...], preferred_element_type=F32))
    idx, gate, rank, counts = _route_tile(scores, br_ref[...], cnt_ref[...])
    e_ref[...] = _columns_to_lanes(idx, jnp.int32)
    gw_ref[...] = _columns_to_lanes(gate, F32)
    rk_ref[...] = _columns_to_lanes(rank, jnp.int32)
    cnt_ref[...] = cnt_ref[...] + counts


def _ln_router(r, g, b, w_router, b_router):
    row = lambda i: (i, 0)
    fixed = lambda i: (0, 0)
    return pl.pallas_call(
        _ln_router_kernel,
        grid=(T_ALL // TM_LN,),
        in_specs=[pl.BlockSpec((TM_LN, D_MODEL), row),
                  pl.BlockSpec((1, D_MODEL), fixed),
                  pl.BlockSpec((1, D_MODEL), fixed),
                  pl.BlockSpec((D_MODEL, N_EXPERTS), fixed),
                  pl.BlockSpec((1, N_EXPERTS), fixed)],
        out_specs=[pl.BlockSpec((TM_LN, D_MODEL), row),
                   pl.BlockSpec((TM_LN, D_MODEL), row),
                   pl.BlockSpec((TM_LN, HALF), row),
                   pl.BlockSpec((TM_LN, LANES), row),
                   pl.BlockSpec((TM_LN, LANES), row),
                   pl.BlockSpec((TM_LN, LANES), row),
                   pl.BlockSpec((1, N_EXPERTS), fixed)],
        out_shape=[jax.ShapeDtypeStruct((T_ALL, D_MODEL), F32),
                   jax.ShapeDtypeStruct((T_ALL, D_MODEL), BF16),
                   jax.ShapeDtypeStruct((T_ALL, HALF), jnp.uint32),
                   jax.ShapeDtypeStruct((T_ALL, LANES), jnp.int32),
                   jax.ShapeDtypeStruct((T_ALL, LANES), F32),
                   jax.ShapeDtypeStruct((T_ALL, LANES), jnp.int32),
                   jax.ShapeDtypeStruct((1, N_EXPERTS), F32)],
        compiler_params=_cparams(("arbitrary",)),
        name="ln_router",
    )(r, g, b, w_router, b_router)


def _ln_kernel(r_ref, g_ref, b_ref, o_ref):
    o_ref[...] = _layer_norm(r_ref[...], g_ref[...], b_ref[...])


def _ln(r, g, b):
    row = lambda i: (i, 0)
    fixed = lambda i: (0, 0)
    return pl.pallas_call(
        _ln_kernel,
        grid=(T_ALL // TM_LN,),
        in_specs=[pl.BlockSpec((TM_LN, D_MODEL), row),
                  pl.BlockSpec((1, D_MODEL), fixed),
                  pl.BlockSpec((1, D_MODEL), fixed)],
        out_specs=pl.BlockSpec((TM_LN, D_MODEL), row),
        out_shape=jax.ShapeDtypeStruct((T_ALL, D_MODEL), F32),
        compiler_params=_cparams(("parallel",)),
        name="ln_final",
    )(r, g, b)


def _moe_kernel(item_e, item_start, item_n, sdst,
                x_hbm, w1_ref, w3_ref, w2_ref, y_hbm,
                xslab, xb, w13b, w2b, act, yslab, gsem, ssem):
    del item_e
    w = pl.program_id(0)
    q = pl.program_id(1)
    n = item_n[w]
    start = item_start[w]
    slot = w % 2
    last_q = MOE_SPLIT - 1

    def x_copy(tok, r, slot_):
        return pltpu.make_async_copy(x_hbm.at[tok], xslab.at[slot_, pl.ds(r * PK_STRIDE, PK_ROWS), :],
                                     gsem.at[slot_])

    def y_copy(r, dst):
        return pltpu.make_async_copy(yslab.at[pl.ds(r * PK_STRIDE, PK_ROWS), :], y_hbm.at[dst], ssem)

    def for_each_row(count, row_fn):
        groups = count // MOE_DMA_UNROLL

        def group_body(g, c):
            for u in range(MOE_DMA_UNROLL):
                row_fn(g * MOE_DMA_UNROLL + u)
            return c

        def tail_body(r, c):
            row_fn(r)
            return c

        lax.fori_loop(0, groups, group_body, 0)
        lax.fori_loop(groups * MOE_DMA_UNROLL, count, tail_body, 0)

    def start_gather(item, slot_):
        base = item_start[item]
        for_each_row(item_n[item],
                     lambda r: x_copy(lax.shift_right_logical(sdst[base + r], TOP_K_SHIFT), r, slot_).start())

    def wait_gather(count, slot_):
        for_each_row(count, lambda r: x_copy(0, r, slot_).wait())

    def wait_scatter(count):
        for_each_row(count, lambda r: y_copy(r, 0).wait())

    @pl.when(jnp.logical_and(w == 0, q == 0))
    def _first():
        xslab[...] = jnp.zeros_like(xslab)
        yslab[...] = jnp.zeros_like(yslab)
        start_gather(0, 0)

    @pl.when(jnp.logical_and(q == 0, n > 0))
    def _wait_rows():
        wait_gather(n, slot)

    @pl.when(jnp.logical_and(q == 1, w + 1 < MOE_ITEMS))
    def _prefetch_rows():
        start_gather(w + 1, 1 - slot)

    @pl.when(jnp.logical_and(q == last_q, w > 0))
    def _wait_prev_scatter():
        wait_scatter(item_n[w - 1])

    @pl.when(n > 0)
    def _compute():
        w13b[:, :MOE_DQ] = w1_ref[...].astype(BF16)
        w13b[:, MOE_DQ:] = w3_ref[...].astype(BF16)
        w2b[pl.ds(pl.multiple_of(q * MOE_DQ, MOE_DQ), MOE_DQ), :] = w2_ref[...].astype(BF16)
        nb = (n + MOE_BUCKET - 1) // MOE_BUCKET
        for b in range(1, MOE_ROWS // MOE_BUCKET + 1):
            rows = b * MOE_BUCKET

            @pl.when(nb == b)
            def _bucket(rows=rows):
                @pl.when(q == 0)
                def _unpack():
                    for s in range(PK_ROWS):
                        hi, lo = _unpack_pair(xslab[slot, pl.ds(s, rows, stride=PK_STRIDE), :])
                        xb[0:rows, s * LANES:(s + 1) * LANES] = hi.astype(BF16)
                        xb[0:rows, HALF + s * LANES:HALF + (s + 1) * LANES] = lo.astype(BF16)

                h = jnp.dot(xb[0:rows, :], w13b[...], preferred_element_type=F32)
                act[q, 0:rows, :] = (jax.nn.silu(h[:, :MOE_DQ]) * h[:, MOE_DQ:]).astype(BF16)

                @pl.when(q == last_q)
                def _down():
                    a = jnp.concatenate([act[i, 0:rows, :] for i in range(MOE_SPLIT)], axis=-1)
                    per_chunk = MOE_NCHUNK // LANES
                    for c in range(HALF // MOE_NCHUNK):
                        lo_cols = slice(c * MOE_NCHUNK, (c + 1) * MOE_NCHUNK)
                        hi_cols = slice(HALF + c * MOE_NCHUNK, HALF + (c + 1) * MOE_NCHUNK)
                        pk = _pack_pair(jnp.dot(a, w2b[:, lo_cols], preferred_element_type=F32),
                                        jnp.dot(a, w2b[:, hi_cols], preferred_element_type=F32))
                        for s in range(per_chunk):
                            yslab[pl.ds(c * per_chunk + s, rows, stride=PK_STRIDE), :] = pk[:, s * LANES:(s + 1) * LANES]

    @pl.when(jnp.logical_and(q == last_q, n > 0))
    def _scatter():
        for_each_row(n, lambda r: y_copy(r, sdst[start + r]).start())

    @pl.when(jnp.logical_and(jnp.logical_and(q == last_q, w == MOE_ITEMS - 1), n > 0))
    def _drain():
        wait_scatter(n)


def _moe_routed(x_pk, w1, w3, w2, item_e, item_start, item_n, sdst):
    def q_eff(w, q, item_n):
        return jnp.where(item_n[w] > 0, q, MOE_SPLIT - 1)

    grid_spec = pltpu.PrefetchScalarGridSpec(
        num_scalar_prefetch=4,
        grid=(MOE_ITEMS, MOE_SPLIT),
        in_specs=[
            pl.BlockSpec(memory_space=pl.ANY),
            pl.BlockSpec((None, D_MODEL, MOE_DQ), lambda w, q, ie, ist, inn, sd: (ie[w], 0, q_eff(w, q, inn))),
            pl.BlockSpec((None, D_MODEL, MOE_DQ), lambda w, q, ie, ist, inn, sd: (ie[w], 0, q_eff(w, q, inn))),
            pl.BlockSpec((None, MOE_DQ, D_MODEL), lambda w, q, ie, ist, inn, sd: (ie[w], q_eff(w, q, inn), 0)),
        ],
        out_specs=pl.BlockSpec(memory_space=pl.ANY),
        scratch_shapes=[
            pltpu.VMEM((2, MOE_ROWS * PK_STRIDE, LANES), jnp.uint32),
            pltpu.VMEM((MOE_ROWS, D_MODEL), BF16),
            pltpu.VMEM((D_MODEL, 2 * MOE_DQ), BF16),
            pltpu.VMEM((D_EXPERT, D_MODEL), BF16),
            pltpu.VMEM((MOE_SPLIT, MOE_ROWS, MOE_DQ), BF16),
            pltpu.VMEM((MOE_ROWS * PK_STRIDE, LANES), jnp.uint32),
            pltpu.SemaphoreType.DMA((2,)),
            pltpu.SemaphoreType.DMA(()),
        ],
    )
    return pl.pallas_call(
        _moe_kernel,
        grid_spec=grid_spec,
        out_shape=jax.ShapeDtypeStruct((N_ASSIGN, PK_ROWS, LANES), jnp.uint32),
        compiler_params=_cparams(("arbitrary", "arbitrary")),
        name="moe_routed",
    )(item_e, item_start, item_n, sdst, x_pk, w1, w3, w2)


def _shared_kernel(x_ref, w1_ref, w3_ref, w2_ref, o_ref):
    x = x_ref[...]
    a = jax.nn.silu(jnp.dot(x, w1_ref[...], preferred_element_type=F32))
    a = (a * jnp.dot(x, w3_ref[...], preferred_element_type=F32)).astype(BF16)
    o_ref[...] = jnp.dot(a, w2_ref[...], preferred_element_type=F32).astype(o_ref.dtype)


def _shared_expert(xb, ws1, ws3, ws2):
    fixed = lambda i: (0, 0)
    return pl.pallas_call(
        _shared_kernel,
        grid=(T_ALL // TM_SHARED,),
        in_specs=[pl.BlockSpec((TM_SHARED, D_MODEL), lambda i: (i, 0)),
                  pl.BlockSpec((D_MODEL, D_SHARED), fixed),
                  pl.BlockSpec((D_MODEL, D_SHARED), fixed),
                  pl.BlockSpec((D_SHARED, D_MODEL), fixed)],
        out_specs=pl.BlockSpec((TM_SHARED, D_MODEL), lambda i: (i, 0)),
        out_shape=jax.ShapeDtypeStruct((T_ALL, D_MODEL), F32),
        compiler_params=_cparams(("parallel",)),
        name="shared_expert",
    )(xb, ws1, ws3, ws2)


def _combine_kernel(y_ref, gw_ref, x_ref, sh_ref, g_ref, b_ref, o_ref):
    lo_rows, hi_rows = slice(0, PK_ROWS), slice(PK_ROWS, SLAB_ROWS)
    acc_lo = ALPHA * x_ref[:, lo_rows, :] + sh_ref[:, lo_rows, :]
    acc_hi = ALPHA * x_ref[:, hi_rows, :] + sh_ref[:, hi_rows, :]
    for k in range(TOP_K):
        y_lo, y_hi = _unpack_pair(y_ref[:, k])
        gate = gw_ref[:, k:k + 1, :]
        acc_lo = acc_lo + y_lo * gate
        acc_hi = acc_hi + y_hi * gate

    def token_sum(a, b):
        return jnp.sum(jnp.sum(a, axis=2, keepdims=True) + jnp.sum(b, axis=2, keepdims=True), axis=1, keepdims=True)

    inv_d = 1.0 / D_MODEL
    mu = token_sum(acc_lo, acc_hi) * inv_d
    c_lo, c_hi = acc_lo - mu, acc_hi - mu
    inv_std = lax.rsqrt(token_sum(c_lo * c_lo, c_hi * c_hi) * inv_d + LN_EPS)
    o_ref[:, lo_rows, :] = c_lo * inv_std * g_ref[:, lo_rows, :] + b_ref[:, lo_rows, :]
    o_ref[:, hi_rows, :] = c_hi * inv_std * g_ref[:, hi_rows, :] + b_ref[:, hi_rows, :]


def _combine(y, gwb, x_slab, sh_slab, g_slab, b_slab):
    tok = lambda i: (i, 0, 0)
    return pl.pallas_call(
        _combine_kernel,
        grid=(T_ALL // TM_COMBINE,),
        in_specs=[pl.BlockSpec((TM_COMBINE, TOP_K, PK_ROWS, LANES), lambda i: (i, 0, 0, 0)),
                  pl.BlockSpec((TM_COMBINE, TOP_K, LANES), tok),
                  pl.BlockSpec((TM_COMBINE, SLAB_ROWS, LANES), tok),
                  pl.BlockSpec((TM_COMBINE, SLAB_ROWS, LANES), tok),
                  pl.BlockSpec((1, SLAB_ROWS, LANES), lambda i: (0, 0, 0)),
                  pl.BlockSpec((1, SLAB_ROWS, LANES), lambda i: (0, 0, 0))],
        out_specs=pl.BlockSpec((TM_COMBINE, SLAB_ROWS, LANES), tok),
        out_shape=jax.ShapeDtypeStruct((T_ALL, SLAB_ROWS, LANES), F32),
        compiler_params=_cparams(("parallel",)),
        name="moe_combine_ln",
    )(y, gwb, x_slab, sh_slab, g_slab, b_slab)


def _ple_kernel(xb_ref, wg_ref, p_ref, wp_ref, x_ref, o_ref):
    gate = jax.nn.sigmoid(jnp.dot(xb_ref[...], wg_ref[...], preferred_element_type=F32))
    proj = jnp.dot(p_ref[...], wp_ref[...], preferred_element_type=F32)
    o_ref[...] = ALPHA * x_ref[...] + gate * proj


def _ple(xb, w_gate, p, w_proj, x):
    return pl.pallas_call(
        _ple_kernel,
        grid=(T_ALL // TM, D_MODEL // TN),
        in_specs=[pl.BlockSpec((TM, D_MODEL), lambda m, n: (m, 0)),
                  pl.BlockSpec((D_MODEL, TN), lambda m, n: (0, n)),
                  pl.BlockSpec((TM, PLE_DIM), lambda m, n: (m, 0)),
                  pl.BlockSpec((PLE_DIM, TN), lambda m, n: (0, n)),
                  pl.BlockSpec((TM, TN), lambda m, n: (m, n))],
        out_specs=pl.BlockSpec((TM, TN), lambda m, n: (m, n)),
        out_shape=jax.ShapeDtypeStruct((T_ALL, D_MODEL), F32),
        compiler_params=_cparams(("parallel", "arbitrary")),
        name="ple",
    )(xb, w_gate, p, w_proj, x)


def _dispatch(eidx, rank, counts):
    sstart = jnp.cumsum(counts) - counts
    pos = sstart[eidx] + rank
    sdst = jnp.zeros((N_ASSIGN,), jnp.int32).at[pos.reshape(-1)].set(
        jnp.arange(N_ASSIGN, dtype=jnp.int32), unique_indices=True)
    n_items = (counts + MOE_ROWS - 1) // MOE_ROWS
    item_end = jnp.cumsum(n_items)
    total = item_end[-1]
    w = jnp.arange(MOE_ITEMS, dtype=jnp.int32)
    e_of = jnp.minimum(jnp.searchsorted(item_end, w, side="right"), N_EXPERTS - 1).astype(jnp.int32)
    local = w - (item_end - n_items)[e_of]
    valid = w < total
    e_last = e_of[jnp.maximum(total - 1, 0)]
    item_e = jnp.where(valid, e_of, e_last)
    item_start = jnp.where(valid, sstart[e_of] + local * MOE_ROWS, 0)
    item_n = jnp.where(valid, jnp.clip(counts[e_of] - local * MOE_ROWS, 0, MOE_ROWS), 0)
    return item_e.astype(jnp.int32), item_start.astype(jnp.int32), item_n.astype(jnp.int32), sdst


def kernel(x_prompt, x_sample, state_pool, p_prompt, p_sample, w_in, ln_v_g, ln_v_b, w_s, b_s, w_pool, pool_scale, w_up_a, w_up_b, w_o, ln_g, ln_b, w_router, b_router, w1, w3, w2, ws1, ws3, ws2, w_ple_gate, w_ple_proj):
    def tokens(a_prompt, a_sample):
        d = a_prompt.shape[-1]
        return jnp.concatenate([a_prompt.reshape(T_PROMPT, d),
                                a_sample.transpose(1, 0, 2).reshape(T_SAMPLE, d)], axis=0)

    x = tokens(x_prompt, x_sample)
    xb = x.astype(BF16)
    pb = tokens(p_prompt[0], p_sample[0]).astype(BF16)
    row2d = lambda v: v.reshape(1, -1).astype(F32)

    w_in_b = w_in[0].astype(BF16)
    ug = _proj(xb, w_in_b, 0, 2 * CH_A, "gelu", BF16)
    z = _proj(xb, w_in_b, 2 * CH_A, CH_B, "none", F32)
    gates = _proj(xb, w_in_b, 2 * CH_A + CH_B, 2 * D_MODEL, "sigmoid", BF16)

    bias_slab = jnp.repeat(b_s[0].T, HEAD_DIM_A, axis=1).astype(F32)
    wexp = jnp.repeat(w_s[0][:, :DEC_SEQ, :DEC_SEQ].transpose(1, 2, 0).reshape(DEC_SEQ * DEC_SEQ, HEADS_A),
                      HEAD_DIM_A, axis=1).astype(F32)
    lvg, lvb = row2d(ln_v_g[0]), row2d(ln_v_b[0])
    oa_p = _spatial_prompt(ug, w_s[0], bias_slab, lvg, lvb)
    oa_s, vn_s = _spatial_sample(ug, wexp, bias_slab, lvg, lvb)
    o_a = jnp.concatenate([oa_p, oa_s], axis=0)

    w_pool_b = w_pool[0].astype(BF16)
    pscale = row2d(pool_scale[0])
    buf_t = state_pool[0].transpose(1, 0, 2)
    ob_p = _pool_prompt(z, w_pool_b, pscale)
    ob_s = _pool_sample(z, buf_t, w_pool_b, pscale)
    o_b = jnp.concatenate([ob_p, ob_s], axis=0)

    m = _merge(o_a, o_b, w_up_a[0].astype(BF16), w_up_b[0].astype(BF16), gates)
    r1 = _out_proj(m, w_o[0].astype(BF16), x)
    x1, x1b, x1pk, e_pad, gw_pad, rank_pad, counts = _ln_router(
        r1, row2d(ln_g[0, 0]), row2d(ln_b[0, 0]), w_router[0].astype(BF16), row2d(b_router[0]))

    eidx, gw, rank = e_pad[:, :TOP_K], gw_pad[:, :TOP_K], rank_pad[:, :TOP_K]
    item_e, item_start, item_n, sdst = _dispatch(eidx, rank, counts[0].astype(jnp.int32))
    x1_slab = x1.reshape(T_ALL, SLAB_ROWS, LANES)
    y = _moe_routed(x1pk.reshape(T_ALL, PK_ROWS, LANES), w1[0], w3[0], w2[0], item_e, item_start, item_n, sdst)
    sh = _shared_expert(x1b, ws1[0].astype(BF16), ws3[0].astype(BF16), ws2[0].astype(BF16))
    gwb = jnp.broadcast_to(gw.reshape(T_ALL, TOP_K, 1), (T_ALL, TOP_K, LANES))
    slab = lambda v: v.reshape(1, SLAB_ROWS, LANES).astype(F32)
    x2 = _combine(y.reshape(T_ALL, TOP_K, PK_ROWS, LANES), gwb, x1_slab,
                  sh.reshape(T_ALL, SLAB_ROWS, LANES), slab(ln_g[0, 1]), slab(ln_b[0, 1]))
    x2 = x2.reshape(T_ALL, D_MODEL)

    r3 = _ple(x2.astype(BF16), w_ple_gate[0].astype(BF16), pb, w_ple_proj[0].astype(BF16), x2)
    x3 = _ln(r3, row2d(ln_g[0, 2]), row2d(ln_b[0, 2]))

    y_prompt = x3[:T_PROMPT].reshape(BATCH, SEQ, D_MODEL)
    y_sample = x3[T_PROMPT:].reshape(DEC_SEQ, DEC_BATCH, D_MODEL).transpose(1, 0, 2)
    z_p = z[:T_PROMPT].reshape(BATCH, SEQ, CH_B)
    z_s = z[T_PROMPT:].reshape(DEC_SEQ, DEC_BATCH, CH_B).transpose(1, 0, 2)
    new_pool_prompt = z_p[:, SEQ - POOL_BUF:][None]
    new_pool_sample = jnp.concatenate([state_pool[0][:, DEC_SEQ:], z_s], axis=1)[None]
    new_chunk_v_sample = vn_s.reshape(DEC_SEQ, DEC_BATCH, CH_A).transpose(1, 0, 2)[None]
    return (y_prompt, y_sample, new_pool_prompt, new_pool_sample, new_chunk_v_sample)
```

```python
import functools

import jax
import jax.numpy as jnp
from jax import lax
from jax.experimental import pallas as pl
from jax.experimental.pallas import tpu as pltpu

F32 = jnp.float32
BF16 = jnp.bfloat16

D_MODEL = 4096
BATCH = 4
SEQ = 2048
DEC_BATCH = 128
DEC_SEQ = 4
PAST_LEN = 16384
CHUNK = 128
HEAD_DIM_A = 128
HEADS_A = 16
CH_A = 2048
POOL_WINDOWS = (2, 4, 8, 16)
POOL_GROUPS = 4
CH_B = 2048
POOL_GC = 512
POOL_BUF = 15
N_EXPERTS = 256
TOP_K = 8
N_GROUPS = 8
TOPK_GROUPS = 4
D_EXPERT = 512
D_SHARED = 512
ROUTE_SCALE = 2.5
PLE_DIM = 256
ALPHA = 2.0 ** 0.25
LN_EPS = 1e-5

T_PROMPT = BATCH * SEQ
T_SAMPLE = DEC_BATCH * DEC_SEQ
T_ALL = T_PROMPT + T_SAMPLE
N_ASSIGN = T_ALL * TOP_K

LANES = 128
SLAB_ROWS = D_MODEL // LANES
VMEM_LIMIT = 56 * 1024 * 1024

TM = 1088
TN = 512
TM_LN = 256
TM_SHARED = 256
TM_COMBINE = 64
POOL_TM = 512
POOL_HALO = 16
MOE_ROWS = 512
MOE_BUCKET = 64
MOE_SPLIT = 4
MOE_DQ = D_EXPERT // MOE_SPLIT
MOE_KQ = D_MODEL // MOE_SPLIT
MOE_KSUB = 512
MOE_ITEMS = N_EXPERTS + N_ASSIGN // MOE_ROWS
HALF = D_MODEL // 2
PK_ROWS = HALF // LANES
PK_STRIDE = 24
MOE_NCHUNK = 512
MOE_DMA_UNROLL = 8
TOP_K_SHIFT = 3
HI_MASK = 0xFFFF0000


def _pack_pair(hi, lo):
    h = lax.bitcast_convert_type(hi.astype(BF16).astype(F32), jnp.uint32)
    l = lax.bitcast_convert_type(lo.astype(BF16).astype(F32), jnp.uint32)
    return h | (l >> jnp.uint32(16))


def _unpack_pair(w):
    hi = lax.bitcast_convert_type(w & jnp.uint32(HI_MASK), F32)
    lo = lax.bitcast_convert_type(w << jnp.uint32(16), F32)
    return hi, lo


def _cparams(sem):
    return pltpu.CompilerParams(dimension_semantics=sem, vmem_limit_bytes=VMEM_LIMIT)


def _layer_norm(xf, g, b):
    mu = jnp.mean(xf, axis=-1, keepdims=True)
    xc = xf - mu
    var = jnp.mean(xc * xc, axis=-1, keepdims=True)
    return xc * lax.rsqrt(var + LN_EPS) * g + b


def _proj_kernel(x_ref, w_ref, o_ref, *, act):
    h = jnp.dot(x_ref[...], w_ref[...], preferred_element_type=F32)
    if act == "gelu":
        h = jax.nn.gelu(h, approximate=True)
    elif act == "sigmoid":
        h = jax.nn.sigmoid(h)
    o_ref[...] = h.astype(o_ref.dtype)


def _proj(x, w, col0, ncols, act, out_dtype):
    t, k = x.shape
    off = col0 // TN
    return pl.pallas_call(
        functools.partial(_proj_kernel, act=act),
        grid=(t // TM, ncols // TN),
        in_specs=[pl.BlockSpec((TM, k), lambda m, n: (m, 0)),
                  pl.BlockSpec((k, TN), lambda m, n: (0, n + off))],
        out_specs=pl.BlockSpec((TM, TN), lambda m, n: (m, n)),
        out_shape=jax.ShapeDtypeStruct((t, ncols), out_dtype),
        compiler_params=_cparams(("parallel", "arbitrary")),
        name="in_proj_" + act,
    )(x, w)


def _spatial_prompt_kernel(u_ref, gv_ref, ws_ref, bias_ref, g_ref, b_ref, o_ref):
    vn = _layer_norm(gv_ref[...].astype(F32), g_ref[...], b_ref[...])
    row = lax.broadcasted_iota(jnp.int32, (CHUNK, CHUNK), 0)
    col = lax.broadcasted_iota(jnp.int32, (CHUNK, CHUNK), 1)
    causal = col <= row
    for h in range(HEADS_A):
        sl = slice(h * HEAD_DIM_A, (h + 1) * HEAD_DIM_A)
        w = jnp.where(causal, ws_ref[h], 0.0).astype(BF16)
        s = jnp.dot(w, vn[:, sl].astype(BF16), preferred_element_type=F32) + bias_ref[:, sl]
        o_ref[:, sl] = (u_ref[:, sl].astype(F32) * s).astype(o_ref.dtype)


def _spatial_prompt(ug, w_s, bias_slab, ln_g, ln_b):
    n_chunks = T_PROMPT // CHUNK
    return pl.pallas_call(
        _spatial_prompt_kernel,
        grid=(n_chunks,),
        in_specs=[pl.BlockSpec((CHUNK, CH_A), lambda c: (c, 0)),
                  pl.BlockSpec((CHUNK, CH_A), lambda c: (c, 1)),
                  pl.BlockSpec((HEADS_A, CHUNK, CHUNK), lambda c: (0, 0, 0)),
                  pl.BlockSpec((CHUNK, CH_A), lambda c: (0, 0)),
                  pl.BlockSpec((1, CH_A), lambda c: (0, 0)),
                  pl.BlockSpec((1, CH_A), lambda c: (0, 0))],
        out_specs=pl.BlockSpec((CHUNK, CH_A), lambda c: (c, 0)),
        out_shape=jax.ShapeDtypeStruct((T_PROMPT, CH_A), BF16),
        compiler_params=_cparams(("parallel",)),
        name="spatial_prompt",
    )(ug, ug, w_s, bias_slab, ln_g, ln_b)


def _spatial_sample_kernel(u_ref, gv_ref, wexp_ref, bias_ref, g_ref, b_ref, o_ref, vn_ref):
    vn_ref[...] = _layer_norm(gv_ref[...].astype(F32), g_ref[...], b_ref[...])
    for t in range(DEC_SEQ):
        rows = slice(t * DEC_BATCH, (t + 1) * DEC_BATCH)
        s = jnp.broadcast_to(bias_ref[t:t + 1, :], (DEC_BATCH, CH_A))
        for j in range(t + 1):
            r = t * DEC_SEQ + j
            s = s + wexp_ref[r:r + 1, :] * vn_ref[j * DEC_BATCH:(j + 1) * DEC_BATCH, :]
        o_ref[rows, :] = (u_ref[rows, :].astype(F32) * s).astype(o_ref.dtype)


def _spatial_sample(ug, wexp, bias_slab, ln_g, ln_b):
    blk = T_PROMPT // T_SAMPLE
    return pl.pallas_call(
        _spatial_sample_kernel,
        grid=(1,),
        in_specs=[pl.BlockSpec((T_SAMPLE, CH_A), lambda i: (blk, 0)),
                  pl.BlockSpec((T_SAMPLE, CH_A), lambda i: (blk, 1)),
                  pl.BlockSpec((DEC_SEQ * DEC_SEQ, CH_A), lambda i: (0, 0)),
                  pl.BlockSpec((CHUNK, CH_A), lambda i: (0, 0)),
                  pl.BlockSpec((1, CH_A), lambda i: (0, 0)),
                  pl.BlockSpec((1, CH_A), lambda i: (0, 0))],
        out_specs=[pl.BlockSpec((T_SAMPLE, CH_A), lambda i: (0, 0)),
                   pl.BlockSpec((T_SAMPLE, CH_A), lambda i: (0, 0))],
        out_shape=[jax.ShapeDtypeStruct((T_SAMPLE, CH_A), BF16),
                   jax.ShapeDtypeStruct((T_SAMPLE, CH_A), F32)],
        compiler_params=_cparams(("arbitrary",)),
        name="spatial_sample",
    )(ug, ug, wexp, bias_slab, ln_g, ln_b)


def _pool_project(d_of_group, wp_ref, scale_ref, o_ref):
    for g in range(POOL_GROUPS):
        sl = slice(g * POOL_GC, (g + 1) * POOL_GC)
        y = jnp.dot(d_of_group(g).astype(BF16), wp_ref[g], preferred_element_type=F32)
        o_ref[:, sl] = (y * scale_ref[:, sl]).astype(o_ref.dtype)


def _pool_prompt_kernel(z_ref, halo_ref, wp_ref, scale_ref, o_ref, cat_ref):
    i = pl.program_id(1)
    halo = halo_ref[...]
    cat_ref[0:POOL_HALO, :] = jnp.where(i == 0, jnp.zeros_like(halo), halo)
    cat_ref[POOL_HALO:, :] = z_ref[...]
    pos = i * POOL_TM + lax.broadcasted_iota(jnp.int32, (POOL_TM, POOL_GC), 0)

    def d_of_group(g):
        win = POOL_WINDOWS[g]
        sl = slice(g * POOL_GC, (g + 1) * POOL_GC)
        acc = cat_ref[POOL_HALO:POOL_HALO + POOL_TM, sl]
        for k in range(1, win):
            acc = acc + cat_ref[POOL_HALO - k:POOL_HALO - k + POOL_TM, sl]
        cnt = jnp.minimum(pos + 1, win).astype(F32)
        return acc / cnt - z_ref[:, sl]

    _pool_project(d_of_group, wp_ref, scale_ref, o_ref)


def _pool_prompt(z, w_pool, pool_scale):
    tiles = SEQ // POOL_TM
    per_tile = POOL_TM // POOL_HALO
    return pl.pallas_call(
        _pool_prompt_kernel,
        grid=(BATCH, tiles),
        in_specs=[pl.BlockSpec((POOL_TM, CH_B), lambda b, i: (b * tiles + i, 0)),
                  pl.BlockSpec((POOL_HALO, CH_B),
                               lambda b, i: (jnp.maximum((b * tiles + i) * per_tile - 1, 0), 0)),
                  pl.BlockSpec((POOL_GROUPS, POOL_GC, POOL_GC), lambda b, i: (0, 0, 0)),
                  pl.BlockSpec((1, CH_B), lambda b, i: (0, 0))],
        out_specs=pl.BlockSpec((POOL_TM, CH_B), lambda b, i: (b * tiles + i, 0)),
        out_shape=jax.ShapeDtypeStruct((T_PROMPT, CH_B), BF16),
        scratch_shapes=[pltpu.VMEM((POOL_HALO + POOL_TM, CH_B), F32)],
        compiler_params=_cparams(("parallel", "arbitrary")),
        name="pool_prompt",
    )(z, z, w_pool, pool_scale)


def _pool_sample_kernel(z_ref, buf_ref, wp_ref, scale_ref, o_ref, d_ref):
    def cat_row(r, sl):
        if r < POOL_BUF:
            return buf_ref[r, :, sl]
        t = r - POOL_BUF
        return z_ref[t * DEC_BATCH:(t + 1) * DEC_BATCH, sl]

    for g in range(POOL_GROUPS):
        win = POOL_WINDOWS[g]
        sl = slice(g * POOL_GC, (g + 1) * POOL_GC)
        for t in range(DEC_SEQ):
            acc = cat_row(POOL_BUF + t, sl)
            for k in range(1, win):
                acc = acc + cat_row(POOL_BUF + t - k, sl)
            cnt = float(min(PAST_LEN + t + 1, win))
            d_ref[t * DEC_BATCH:(t + 1) * DEC_BATCH, sl] = acc / cnt - cat_row(POOL_BUF + t, sl)

    _pool_project(lambda g: d_ref[:, g * POOL_GC:(g + 1) * POOL_GC], wp_ref, scale_ref, o_ref)


def _pool_sample(z, buf_t, w_pool, pool_scale):
    blk = T_PROMPT // T_SAMPLE
    return pl.pallas_call(
        _pool_sample_kernel,
        grid=(1,),
        in_specs=[pl.BlockSpec((T_SAMPLE, CH_B), lambda i: (blk, 0)),
                  pl.BlockSpec((POOL_BUF, DEC_BATCH, CH_B), lambda i: (0, 0, 0)),
                  pl.BlockSpec((POOL_GROUPS, POOL_GC, POOL_GC), lambda i: (0, 0, 0)),
                  pl.BlockSpec((1, CH_B), lambda i: (0, 0))],
        out_specs=pl.BlockSpec((T_SAMPLE, CH_B), lambda i: (0, 0)),
        out_shape=jax.ShapeDtypeStruct((T_SAMPLE, CH_B), BF16),
        scratch_shapes=[pltpu.VMEM((T_SAMPLE, CH_B), F32)],
        compiler_params=_cparams(("arbitrary",)),
        name="pool_sample",
    )(z, buf_t, w_pool, pool_scale)


def _merge_kernel(oa_ref, ob_ref, wa_ref, wb_ref, ga_ref, gb_ref, o_ref):
    a = jnp.dot(oa_ref[...], wa_ref[...], preferred_element_type=F32)
    b = jnp.dot(ob_ref[...], wb_ref[...], preferred_element_type=F32)
    o_ref[...] = (ga_ref[...].astype(F32) * a + gb_ref[...].astype(F32) * b).astype(o_ref.dtype)


def _merge(o_a, o_b, w_up_a, w_up_b, gates):
    nb = D_MODEL // TN
    return pl.pallas_call(
        _merge_kernel,
        grid=(T_ALL // TM, nb),
        in_specs=[pl.BlockSpec((TM, CH_A), lambda m, n: (m, 0)),
                  pl.BlockSpec((TM, CH_B), lambda m, n: (m, 0)),
                  pl.BlockSpec((CH_A, TN), lambda m, n: (0, n)),
                  pl.BlockSpec((CH_B, TN), lambda m, n: (0, n)),
                  pl.BlockSpec((TM, TN), lambda m, n: (m, n)),
                  pl.BlockSpec((TM, TN), lambda m, n: (m, n + nb))],
        out_specs=pl.BlockSpec((TM, TN), lambda m, n: (m, n)),
        out_shape=jax.ShapeDtypeStruct((T_ALL, D_MODEL), BF16),
        compiler_params=_cparams(("parallel", "arbitrary")),
        name="merge",
    )(o_a, o_b, w_up_a, w_up_b, gates, gates)


def _out_proj_kernel(m_ref, w_ref, x_ref, o_ref):
    o_ref[...] = ALPHA * x_ref[...] + jnp.dot(m_ref[...], w_ref[...], preferred_element_type=F32)


def _out_proj(m, w_o, x):
    return pl.pallas_call(
        _out_proj_kernel,
        grid=(T_ALL // TM, D_MODEL // TN),
        in_specs=[pl.BlockSpec((TM, D_MODEL), lambda m_, n: (m_, 0)),
                  pl.BlockSpec((D_MODEL, TN), lambda m_, n: (0, n)),
                  pl.BlockSpec((TM, TN), lambda m_, n: (m_, n))],
        out_specs=pl.BlockSpec((TM, TN), lambda m_, n: (m_, n)),
        out_shape=jax.ShapeDtypeStruct((T_ALL, D_MODEL), F32),
        compiler_params=_cparams(("parallel", "arbitrary")),
        name="out_proj",
    )(m, w_o, x)


def _route_tile(scores, bias, carry):
    tm = scores.shape[0]
    lane = lax.broadcasted_iota(jnp.int32, (tm, N_EXPERTS), 1)
    per_group = N_EXPERTS // N_GROUPS
    neg = -jnp.inf
    biased = scores + bias

    def first_argmax(v):
        m = jnp.max(v, axis=-1, keepdims=True)
        return m, jnp.min(jnp.where(v == m, lane, N_EXPERTS), axis=-1, keepdims=True)

    in_group = [jnp.logical_and(lane >= g * per_group, lane < (g + 1) * per_group) for g in range(N_GROUPS)]
    gscore = []
    for g in range(N_GROUPS):
        v = jnp.where(in_group[g], biased, neg)
        m1, i1 = first_argmax(v)
        m2 = jnp.max(jnp.where(lane == i1, neg, v), axis=-1, keepdims=True)
        gscore.append(m1 + m2)
    group_kept = jnp.zeros((tm, N_EXPERTS), jnp.int32)
    for g in range(N_GROUPS):
        beaten_by = jnp.zeros((tm, 1), jnp.int32)
        for h in range(N_GROUPS):
            if h == g:
                continue
            wins = gscore[h] >= gscore[g] if h < g else gscore[h] > gscore[g]
            beaten_by = beaten_by + jnp.where(wins, 1, 0)
        kept = jnp.where(beaten_by < TOPK_GROUPS, 1, 0)
        group_kept = group_kept + jnp.where(in_group[g], kept, 0)
    masked = jnp.where(group_kept > 0, biased, neg)

    idx, gate = [], []
    sel_f = jnp.zeros((tm, N_EXPERTS), F32)
    for _ in range(TOP_K):
        _, i = first_argmax(masked)
        hit = lane == i
        idx.append(i)
        gate.append(jnp.sum(jnp.where(hit, scores, 0.0), axis=-1, keepdims=True))
        sel_f = sel_f + jnp.where(hit, 1.0, 0.0)
        masked = jnp.where(hit, neg, masked)
    total = gate[0]
    for k in range(1, TOP_K):
        total = total + gate[k]
    gate = [gk / total * ROUTE_SCALE for gk in gate]

    r_i = lax.broadcasted_iota(jnp.int32, (tm, tm), 0)
    c_i = lax.broadcasted_iota(jnp.int32, (tm, tm), 1)
    lower = jnp.where(c_i < r_i, 1.0, 0.0).astype(BF16)
    rank_full = carry + jnp.dot(lower, sel_f.astype(BF16), preferred_element_type=F32)
    rank = [jnp.sum(jnp.where(lane == i, rank_full, 0.0), axis=-1, keepdims=True) for i in idx]
    return idx, gate, rank, jnp.sum(sel_f, axis=0, keepdims=True)


def _columns_to_lanes(cols, dtype):
    tm = cols[0].shape[0]
    lane = lax.broadcasted_iota(jnp.int32, (tm, LANES), 1)
    out = jnp.zeros((tm, LANES), dtype)
    for k, c in enumerate(cols):
        out = jnp.where(lane == k, c.astype(dtype), out)
    return out


def _ln_router_kernel(r_ref, g_ref, b_ref, wr_ref, br_ref, x_ref, xb_ref, xpk_ref, e_ref, gw_ref, rk_ref, cnt_ref):
    @pl.when(pl.program_id(0) == 0)
    def _init():
        cnt_ref[...] = jnp.zeros_like(cnt_ref)

    x1 = _layer_norm(r_ref[...], g_ref[...], b_ref[...])
    x_ref[...] = x1
    xb = x1.astype(BF16)
    xb_ref[...] = xb
    xpk_ref[...] = _pack_pair(x1[:, :HALF], x1[:, HALF:])
    scores = jax.nn.sigmoid(jnp.dot(xb, wr_ref[...], preferred_element_type=F32))
    idx, gate, rank, counts = _route_tile(scores, br_ref[...], cnt_ref[...])
    e_ref[...] = _columns_to_lanes(idx, jnp.int32)
    gw_ref[...] = _columns_to_lanes(gate, F32)
    rk_ref[...] = _columns_to_lanes(rank, jnp.int32)
    cnt_ref[...] = cnt_ref[...] + counts


def _ln_router(r, g, b, w_router, b_router):
    row = lambda i: (i, 0)
    fixed = lambda i: (0, 0)
    return pl.pallas_call(
        _ln_router_kernel,
        grid=(T_ALL // TM_LN,),
        in_specs=[pl.BlockSpec((TM_LN, D_MODEL), row),
                  pl.BlockSpec((1, D_MODEL), fixed),
                  pl.BlockSpec((1, D_MODEL), fixed),
                  pl.BlockSpec((D_MODEL, N_EXPERTS), fixed),
                  pl.BlockSpec((1, N_EXPERTS), fixed)],
        out_specs=[pl.BlockSpec((TM_LN, D_MODEL), row),
                   pl.BlockSpec((TM_LN, D_MODEL), row),
                   pl.BlockSpec((TM_LN, HALF), row),
                   pl.BlockSpec((TM_LN, LANES), row),
                   pl.BlockSpec((TM_LN, LANES), row),
                   pl.BlockSpec((TM_LN, LANES), row),
                   pl.BlockSpec((1, N_EXPERTS), fixed)],
        out_shape=[jax.ShapeDtypeStruct((T_ALL, D_MODEL), F32),
                   jax.ShapeDtypeStruct((T_ALL, D_MODEL), BF16),
                   jax.ShapeDtypeStruct((T_ALL, HALF), jnp.uint32),
                   jax.ShapeDtypeStruct((T_ALL, LANES), jnp.int32),
                   jax.ShapeDtypeStruct((T_ALL, LANES), F32),
                   jax.ShapeDtypeStruct((T_ALL, LANES), jnp.int32),
                   jax.ShapeDtypeStruct((1, N_EXPERTS), F32)],
        compiler_params=_cparams(("arbitrary",)),
        name="ln_router",
    )(r, g, b, w_router, b_router)


def _ln_split_kernel(r_ref, g_ref, b_ref, op_ref, os_ref):
    i = pl.program_id(0)
    y = _layer_norm(r_ref[...], g_ref[...], b_ref[...])

    @pl.when(i < T_PROMPT // TM_LN)
    def _prompt():
        op_ref[...] = y

    @pl.when(i >= T_PROMPT // TM_LN)
    def _sample():
        os_ref[...] = y


def _ln_split(r, g, b):
    fixed = lambda i: (0, 0)
    n_prompt = T_PROMPT // TM_LN
    return pl.pallas_call(
        _ln_split_kernel,
        grid=(T_ALL // TM_LN,),
        in_specs=[pl.BlockSpec((TM_LN, D_MODEL), lambda i: (i, 0)),
                  pl.BlockSpec((1, D_MODEL), fixed),
                  pl.BlockSpec((1, D_MODEL), fixed)],
        out_specs=[pl.BlockSpec((TM_LN, D_MODEL), lambda i: (jnp.minimum(i, n_prompt - 1), 0)),
                   pl.BlockSpec((TM_LN, D_MODEL), lambda i: (jnp.maximum(i - n_prompt, 0), 0))],
        out_shape=[jax.ShapeDtypeStruct((T_PROMPT, D_MODEL), F32),
                   jax.ShapeDtypeStruct((T_SAMPLE, D_MODEL), F32)],
        compiler_params=_cparams(("arbitrary",)),
        name="ln_final",
    )(r, g, b)


def _moe_kernel(item_e, item_start, item_n, sdst,
                x_hbm, w1_ref, w3_ref, w2_ref, y_hbm,
                xslab, xb, w13b, w2b, hacc, yslab, gsem, ssem):
    del item_e
    w = pl.program_id(0)
    q = pl.program_id(1)
    n = item_n[w]
    start = item_start[w]
    slot = w % 2
    last_q = MOE_SPLIT - 1

    def x_copy(tok, r, slot_):
        return pltpu.make_async_copy(x_hbm.at[tok], xslab.at[slot_, pl.ds(r * PK_STRIDE, PK_ROWS), :],
                                     gsem.at[slot_])

    def y_copy(r, dst):
        return pltpu.make_async_copy(yslab.at[pl.ds(r * PK_STRIDE, PK_ROWS), :], y_hbm.at[dst], ssem)

    def for_each_row(count, row_fn):
        groups = count // MOE_DMA_UNROLL

        def group_body(g, c):
            for u in range(MOE_DMA_UNROLL):
                row_fn(g * MOE_DMA_UNROLL + u)
            return c

        def tail_body(r, c):
            row_fn(r)
            return c

        lax.fori_loop(0, groups, group_body, 0)
        lax.fori_loop(groups * MOE_DMA_UNROLL, count, tail_body, 0)

    def start_gather(item, slot_):
        base = item_start[item]
        for_each_row(item_n[item],
                     lambda r: x_copy(lax.shift_right_logical(sdst[base + r], TOP_K_SHIFT), r, slot_).start())

    def wait_gather(count, slot_):
        for_each_row(count, lambda r: x_copy(0, r, slot_).wait())

    def wait_scatter(count):
        for_each_row(count, lambda r: y_copy(r, 0).wait())

    @pl.when(jnp.logical_and(w == 0, q == 0))
    def _first():
        xslab[...] = jnp.zeros_like(xslab)
        yslab[...] = jnp.zeros_like(yslab)
        hacc[...] = jnp.zeros_like(hacc)
        start_gather(0, 0)

    @pl.when(jnp.logical_and(q == 0, n > 0))
    def _wait_rows():
        wait_gather(n, slot)

    @pl.when(jnp.logical_and(q == 1, w + 1 < MOE_ITEMS))
    def _prefetch_rows():
        start_gather(w + 1, 1 - slot)

    @pl.when(jnp.logical_and(q == last_q, w > 0))
    def _wait_prev_scatter():
        wait_scatter(item_n[w - 1])

    @pl.when(n > 0)
    def _compute():
        nb = (n + MOE_BUCKET - 1) // MOE_BUCKET
        for b in range(1, MOE_ROWS // MOE_BUCKET + 1):
            rows = b * MOE_BUCKET

            @pl.when(nb == b)
            def _bucket(rows=rows):
                @pl.when(q == 0)
                def _unpack():
                    for s in range(PK_ROWS):
                        hi, lo = _unpack_pair(xslab[slot, pl.ds(s, rows, stride=PK_STRIDE), :])
                        for col, val in ((s * LANES, hi), (HALF + s * LANES, lo)):
                            xb[col // MOE_KQ, 0:rows, col % MOE_KQ:col % MOE_KQ + LANES] = val.astype(BF16)

                part = None
                for c in range(MOE_KQ // MOE_KSUB):
                    ks = slice(c * MOE_KSUB, (c + 1) * MOE_KSUB)
                    w13b[ks, :D_EXPERT] = w1_ref[ks, :].astype(BF16)
                    w13b[ks, D_EXPERT:] = w3_ref[ks, :].astype(BF16)
                    d = jnp.dot(xb[q, 0:rows, ks], w13b[ks, :], preferred_element_type=F32)
                    part = d if part is None else part + d
                h = jnp.where(q == 0, part, hacc[0:rows, :] + part)
                hacc[0:rows, :] = h
                w2b[pl.ds(pl.multiple_of(q * MOE_DQ, MOE_DQ), MOE_DQ), :] = w2_ref[...].astype(BF16)

                @pl.when(q == last_q)
                def _down():
                    a = (jax.nn.silu(h[:, :D_EXPERT]) * h[:, D_EXPERT:]).astype(BF16)
                    per_chunk = MOE_NCHUNK // LANES
                    for c in range(HALF // MOE_NCHUNK):
                        lo_cols = slice(c * MOE_NCHUNK, (c + 1) * MOE_NCHUNK)
                        hi_cols = slice(HALF + c * MOE_NCHUNK, HALF + (c + 1) * MOE_NCHUNK)
                        pk = _pack_pair(jnp.dot(a, w2b[:, lo_cols], preferred_element_type=F32),
                                        jnp.dot(a, w2b[:, hi_cols], preferred_element_type=F32))
                        for s in range(per_chunk):
                            yslab[pl.ds(c * per_chunk + s, rows, stride=PK_STRIDE), :] = pk[:, s * LANES:(s + 1) * LANES]

    @pl.when(jnp.logical_and(q == last_q, n > 0))
    def _scatter():
        for_each_row(n, lambda r: y_copy(r, sdst[start + r]).start())

    @pl.when(jnp.logical_and(jnp.logical_and(q == last_q, w == MOE_ITEMS - 1), n > 0))
    def _drain():
        wait_scatter(n)


def _moe_routed(x_pk, w1, w3, w2, item_e, item_start, item_n, sdst):
    def q_eff(w, q, item_n):
        return jnp.where(item_n[w] > 0, q, MOE_SPLIT - 1)

    grid_spec = pltpu.PrefetchScalarGridSpec(
        num_scalar_prefetch=4,
        grid=(MOE_ITEMS, MOE_SPLIT),
        in_specs=[
            pl.BlockSpec(memory_space=pl.ANY),
            pl.BlockSpec((None, MOE_KQ, D_EXPERT), lambda w, q, ie, ist, inn, sd: (ie[w], q_eff(w, q, inn), 0)),
            pl.BlockSpec((None, MOE_KQ, D_EXPERT), lambda w, q, ie, ist, inn, sd: (ie[w], q_eff(w, q, inn), 0)),
            pl.BlockSpec((None, MOE_DQ, D_MODEL), lambda w, q, ie, ist, inn, sd: (ie[w], q_eff(w, q, inn), 0)),
        ],
        out_specs=pl.BlockSpec(memory_space=pl.ANY),
        scratch_shapes=[
            pltpu.VMEM((2, MOE_ROWS * PK_STRIDE, LANES), jnp.uint32),
            pltpu.VMEM((MOE_SPLIT, MOE_ROWS, MOE_KQ), BF16),
            pltpu.VMEM((MOE_KQ, 2 * D_EXPERT), BF16),
            pltpu.VMEM((D_EXPERT, D_MODEL), BF16),
            pltpu.VMEM((MOE_ROWS, 2 * D_EXPERT), F32),
            pltpu.VMEM((MOE_ROWS * PK_STRIDE, LANES), jnp.uint32),
            pltpu.SemaphoreType.DMA((2,)),
            pltpu.SemaphoreType.DMA(()),
        ],
    )
    return pl.pallas_call(
        _moe_kernel,
        grid_spec=grid_spec,
        out_shape=jax.ShapeDtypeStruct((N_ASSIGN, PK_ROWS, LANES), jnp.uint32),
        compiler_params=_cparams(("arbitrary", "arbitrary")),
        name="moe_routed",
    )(item_e, item_start, item_n, sdst, x_pk, w1, w3, w2)


def _shared_kernel(x_ref, xb_ref, w1_ref, w3_ref, w2_ref, o_ref):
    xb = xb_ref[...]
    a = jax.nn.silu(jnp.dot(xb, w1_ref[...], preferred_element_type=F32))
    a = (a * jnp.dot(xb, w3_ref[...], preferred_element_type=F32)).astype(BF16)
    o_ref[...] = ALPHA * x_ref[...] + jnp.dot(a, w2_ref[...], preferred_element_type=F32)


def _shared_expert(x, xb, ws1, ws3, ws2):
    fixed = lambda i: (0, 0)
    return pl.pallas_call(
        _shared_kernel,
        grid=(T_ALL // TM_SHARED,),
        in_specs=[pl.BlockSpec((TM_SHARED, D_MODEL), lambda i: (i, 0)),
                  pl.BlockSpec((TM_SHARED, D_MODEL), lambda i: (i, 0)),
                  pl.BlockSpec((D_MODEL, D_SHARED), fixed),
                  pl.BlockSpec((D_MODEL, D_SHARED), fixed),
                  pl.BlockSpec((D_SHARED, D_MODEL), fixed)],
        out_specs=pl.BlockSpec((TM_SHARED, D_MODEL), lambda i: (i, 0)),
        out_shape=jax.ShapeDtypeStruct((T_ALL, D_MODEL), F32),
        compiler_params=_cparams(("parallel",)),
        name="shared_expert",
    )(x, xb, ws1, ws3, ws2)


def _combine_kernel(y_ref, gw_ref, base_ref, g_ref, b_ref, o_ref):
    lo_rows, hi_rows = slice(0, PK_ROWS), slice(PK_ROWS, SLAB_ROWS)
    acc_lo = base_ref[:, lo_rows, :]
    acc_hi = base_ref[:, hi_rows, :]
    for k in range(TOP_K):
        y_lo, y_hi = _unpack_pair(y_ref[:, k])
        gate = gw_ref[:, k:k + 1, :]
        acc_lo = acc_lo + y_lo * gate
        acc_hi = acc_hi + y_hi * gate

    def token_sum(a, b):
        return jnp.sum(jnp.sum(a, axis=2, keepdims=True) + jnp.sum(b, axis=2, keepdims=True), axis=1, keepdims=True)

    inv_d = 1.0 / D_MODEL
    mu = token_sum(acc_lo, acc_hi) * inv_d
    c_lo, c_hi = acc_lo - mu, acc_hi - mu
    inv_std = lax.rsqrt(token_sum(c_lo * c_lo, c_hi * c_hi) * inv_d + LN_EPS)
    o_ref[:, lo_rows, :] = c_lo * inv_std * g_ref[:, lo_rows, :] + b_ref[:, lo_rows, :]
    o_ref[:, hi_rows, :] = c_hi * inv_std * g_ref[:, hi_rows, :] + b_ref[:, hi_rows, :]


def _combine(y, gwb, base_slab, g_slab, b_slab):
    tok = lambda i: (i, 0, 0)
    return pl.pallas_call(
        _combine_kernel,
        grid=(T_ALL // TM_COMBINE,),
        in_specs=[pl.BlockSpec((TM_COMBINE, TOP_K, PK_ROWS, LANES), lambda i: (i, 0, 0, 0)),
                  pl.BlockSpec((TM_COMBINE, TOP_K, LANES), tok),
                  pl.BlockSpec((TM_COMBINE, SLAB_ROWS, LANES), tok),
                  pl.BlockSpec((1, SLAB_ROWS, LANES), lambda i: (0, 0, 0)),
                  pl.BlockSpec((1, SLAB_ROWS, LANES), lambda i: (0, 0, 0))],
        out_specs=pl.BlockSpec((TM_COMBINE, SLAB_ROWS, LANES), tok),
        out_shape=jax.ShapeDtypeStruct((T_ALL, SLAB_ROWS, LANES), F32),
        compiler_params=_cparams(("parallel",)),
        name="moe_combine_ln",
    )(y, gwb, base_slab, g_slab, b_slab)


def _ple_kernel(xb_ref, wg_ref, p_ref, wp_ref, x_ref, o_ref):
    gate = jax.nn.sigmoid(jnp.dot(xb_ref[...], wg_ref[...], preferred_element_type=F32))
    proj = jnp.dot(p_ref[...], wp_ref[...], preferred_element_type=F32)
    o_ref[...] = ALPHA * x_ref[...] + gate * proj


def _ple(xb, w_gate, p, w_proj, x):
    return pl.pallas_call(
        _ple_kernel,
        grid=(T_ALL // TM, D_MODEL // TN),
        in_specs=[pl.BlockSpec((TM, D_MODEL), lambda m, n: (m, 0)),
                  pl.BlockSpec((D_MODEL, TN), lambda m, n: (0, n)),
                  pl.BlockSpec((TM, PLE_DIM), lambda m, n: (m, 0)),
                  pl.BlockSpec((PLE_DIM, TN), lambda m, n: (0, n)),
                  pl.BlockSpec((TM, TN), lambda m, n: (m, n))],
        out_specs=pl.BlockSpec((TM, TN), lambda m, n: (m, n)),
        out_shape=jax.ShapeDtypeStruct((T_ALL, D_MODEL), F32),
        compiler_params=_cparams(("parallel", "arbitrary")),
        name="ple",
    )(xb, w_gate, p, w_proj, x)


def _sorted_pos_kernel(e_ref, rk_ref, start_ref, o_ref):
    tm = e_ref.shape[0]
    lane = lax.broadcasted_iota(jnp.int32, (tm, N_EXPERTS), 1)
    cols = []
    for k in range(TOP_K):
        seg = jnp.sum(jnp.where(lane == e_ref[:, k:k + 1], start_ref[...], 0.0), axis=-1, keepdims=True)
        cols.append(seg.astype(jnp.int32) + rk_ref[:, k:k + 1])
    o_ref[...] = _columns_to_lanes(cols, jnp.int32)


def _sorted_pos(e_pad, rank_pad, seg_start):
    row = lambda i: (i, 0)
    return pl.pallas_call(
        _sorted_pos_kernel,
        grid=(T_ALL // TM,),
        in_specs=[pl.BlockSpec((TM, LANES), row),
                  pl.BlockSpec((TM, LANES), row),
                  pl.BlockSpec((1, N_EXPERTS), lambda i: (0, 0))],
        out_specs=pl.BlockSpec((TM, LANES), row),
        out_shape=jax.ShapeDtypeStruct((T_ALL, LANES), jnp.int32),
        compiler_params=_cparams(("parallel",)),
        name="sorted_pos",
    )(e_pad, rank_pad, seg_start)


def _invert_kernel(pos_ref, o_ref):
    def body(g, c):
        for u in range(MOE_DMA_UNROLL):
            i = g * MOE_DMA_UNROLL + u
            o_ref[pos_ref[i]] = i
        return c

    lax.fori_loop(0, N_ASSIGN // MOE_DMA_UNROLL, body, 0)


def _invert_permutation(pos):
    return pl.pallas_call(
        _invert_kernel,
        grid_spec=pltpu.PrefetchScalarGridSpec(
            num_scalar_prefetch=1, grid=(1,), in_specs=[],
            out_specs=pl.BlockSpec(memory_space=pltpu.SMEM)),
        out_shape=jax.ShapeDtypeStruct((N_ASSIGN,), jnp.int32),
        compiler_params=_cparams(("arbitrary",)),
        name="invert_permutation",
    )(pos)


def _dispatch(e_pad, rank_pad, counts):
    sstart = jnp.cumsum(counts) - counts
    pos = _sorted_pos(e_pad, rank_pad, sstart.astype(F32).reshape(1, N_EXPERTS))
    sdst = _invert_permutation(pos[:, :TOP_K].reshape(N_ASSIGN))
    n_items = (counts + MOE_ROWS - 1) // MOE_ROWS
    item_end = jnp.cumsum(n_items)
    total = item_end[-1]
    w = jnp.arange(MOE_ITEMS, dtype=jnp.int32)
    e_of = jnp.minimum(jnp.searchsorted(item_end, w, side="right"), N_EXPERTS - 1).astype(jnp.int32)
    local = w - (item_end - n_items)[e_of]
    valid = w < total
    e_last = e_of[jnp.maximum(total - 1, 0)]
    item_e = jnp.where(valid, e_of, e_last)
    item_start = jnp.where(valid, sstart[e_of] + local * MOE_ROWS, 0)
    item_n = jnp.where(valid, jnp.clip(counts[e_of] - local * MOE_ROWS, 0, MOE_ROWS), 0)
    return item_e.astype(jnp.int32), item_start.astype(jnp.int32), item_n.astype(jnp.int32), sdst


def kernel(x_prompt, x_sample, state_pool, p_prompt, p_sample, w_in, ln_v_g, ln_v_b, w_s, b_s, w_pool, pool_scale, w_up_a, w_up_b, w_o, ln_g, ln_b, w_router, b_router, w1, w3, w2, ws1, ws3, ws2, w_ple_gate, w_ple_proj):
    def tokens(a_prompt, a_sample):
        d = a_prompt.shape[-1]
        return jnp.concatenate([a_prompt.reshape(T_PROMPT, d),
                                a_sample.transpose(1, 0, 2).reshape(T_SAMPLE, d)], axis=0)

    x = tokens(x_prompt, x_sample)
    xb = x.astype(BF16)
    pb = tokens(p_prompt[0], p_sample[0]).astype(BF16)
    row2d = lambda v: v.reshape(1, -1).astype(F32)

    w_in_b = w_in[0].astype(BF16)
    ug = _proj(xb, w_in_b, 0, 2 * CH_A, "gelu", BF16)
    z = _proj(xb, w_in_b, 2 * CH_A, CH_B, "none", F32)
    gates = _proj(xb, w_in_b, 2 * CH_A + CH_B, 2 * D_MODEL, "sigmoid", BF16)

    bias_slab = jnp.repeat(b_s[0].T, HEAD_DIM_A, axis=1).astype(F32)
    wexp = jnp.repeat(w_s[0][:, :DEC_SEQ, :DEC_SEQ].transpose(1, 2, 0).reshape(DEC_SEQ * DEC_SEQ, HEADS_A),
                      HEAD_DIM_A, axis=1).astype(F32)
    lvg, lvb = row2d(ln_v_g[0]), row2d(ln_v_b[0])
    oa_p = _spatial_prompt(ug, w_s[0], bias_slab, lvg, lvb)
    oa_s, vn_s = _spatial_sample(ug, wexp, bias_slab, lvg, lvb)
    o_a = jnp.concatenate([oa_p, oa_s], axis=0)

    w_pool_b = w_pool[0].astype(BF16)
    pscale = row2d(pool_scale[0])
    buf_t = state_pool[0].transpose(1, 0, 2)
    ob_p = _pool_prompt(z, w_pool_b, pscale)
    ob_s = _pool_sample(z, buf_t, w_pool_b, pscale)
    o_b = jnp.concatenate([ob_p, ob_s], axis=0)

    m = _merge(o_a, o_b, w_up_a[0].astype(BF16), w_up_b[0].astype(BF16), gates)
    r1 = _out_proj(m, w_o[0].astype(BF16), x)
    x1, x1b, x1pk, e_pad, gw_pad, rank_pad, counts = _ln_router(
        r1, row2d(ln_g[0, 0]), row2d(ln_b[0, 0]), w_router[0].astype(BF16), row2d(b_router[0]))

    item_e, item_start, item_n, sdst = _dispatch(e_pad, rank_pad, counts[0].astype(jnp.int32))
    y = _moe_routed(x1pk.reshape(T_ALL, PK_ROWS, LANES), w1[0], w3[0], w2[0], item_e, item_start, item_n, sdst)
    base = _shared_expert(x1, x1b, ws1[0].astype(BF16), ws3[0].astype(BF16), ws2[0].astype(BF16))
    gwb = jnp.broadcast_to(gw_pad[:, :TOP_K].reshape(T_ALL, TOP_K, 1), (T_ALL, TOP_K, LANES))
    slab = lambda v: v.reshape(1, SLAB_ROWS, LANES).astype(F32)
    x2 = _combine(y.reshape(T_ALL, TOP_K, PK_ROWS, LANES), gwb, base.reshape(T_ALL, SLAB_ROWS, LANES),
                  slab(ln_g[0, 1]), slab(ln_b[0, 1]))
    x2 = x2.reshape(T_ALL, D_MODEL)

    r3 = _ple(x2.astype(BF16), w_ple_gate[0].astype(BF16), pb, w_ple_proj[0].astype(BF16), x2)
    y_p, y_s = _ln_split(r3, row2d(ln_g[0, 2]), row2d(ln_b[0, 2]))

    y_prompt = y_p.reshape(BATCH, SEQ, D_MODEL)
    y_sample = y_s.reshape(DEC_SEQ, DEC_BATCH, D_MODEL).transpose(1, 0, 2)
    z_s = z[T_PROMPT:].reshape(DEC_SEQ, DEC_BATCH, CH_B).transpose(1, 0, 2)
    new_pool_prompt = jnp.stack([z[(b + 1) * SEQ - POOL_BUF:(b + 1) * SEQ] for b in range(BATCH)])[None]
    new_pool_sample = jnp.concatenate([state_pool[0][:, DEC_SEQ:], z_s], axis=1)[None]
    new_chunk_v_sample = vn_s.reshape(DEC_SEQ, DEC_BATCH, CH_A).transpose(1, 0, 2)[None]
    return (y_prompt, y_sample, new_pool_prompt, new_pool_sample, new_chunk_v_sample)
```

```python
import functools

import jax
import jax.numpy as jnp
from jax import lax
from jax.experimental import pallas as pl
from jax.experimental.pallas import tpu as pltpu

F32 = jnp.float32
BF16 = jnp.bfloat16

D_MODEL = 4096
BATCH = 4
SEQ = 2048
DEC_BATCH = 128
DEC_SEQ = 4
PAST_LEN = 16384
CHUNK = 128
HEAD_DIM_A = 128
HEADS_A = 16
CH_A = 2048
POOL_WINDOWS = (2, 4, 8, 16)
POOL_GROUPS = 4
CH_B = 2048
POOL_GC = 512
POOL_BUF = 15
N_EXPERTS = 256
TOP_K = 8
N_GROUPS = 8
TOPK_GROUPS = 4
D_EXPERT = 512
D_SHARED = 512
ROUTE_SCALE = 2.5
PLE_DIM = 256
ALPHA = 2.0 ** 0.25
LN_EPS = 1e-5

T_PROMPT = BATCH * SEQ
T_SAMPLE = DEC_BATCH * DEC_SEQ
T_ALL = T_PROMPT + T_SAMPLE
N_ASSIGN = T_ALL * TOP_K

LANES = 128
SLAB_ROWS = D_MODEL // LANES
SLAB_STRIDE = 40
VMEM_LIMIT = 56 * 1024 * 1024

TM = 1088
TN = 512
TM_OUT = 512
TM_LN = 256
TM_SHARED = 256
TM_COMBINE = 64
POOL_TM = 512
POOL_HALO = 16
MOE_ROWS = 512
MOE_BUCKET = 64
MOE_SPLIT = 4
MOE_DQ = D_EXPERT // MOE_SPLIT
MOE_KQ = D_MODEL // MOE_SPLIT
MOE_KSUB = 512
MOE_DMA_PARTS = 4
MOE_ITEMS = N_EXPERTS + N_ASSIGN // MOE_ROWS
HALF = D_MODEL // 2
PK_ROWS = HALF // LANES
PK_STRIDE = 24
MOE_NCHUNK = 512
MOE_DMA_UNROLL = 8
TOP_K_SHIFT = 3
HI_MASK = 0xFFFF0000


def _pack_pair(hi, lo):
    h = lax.bitcast_convert_type(hi.astype(BF16).astype(F32), jnp.uint32)
    l = lax.bitcast_convert_type(lo.astype(BF16).astype(F32), jnp.uint32)
    return h | (l >> jnp.uint32(16))


def _unpack_pair(w):
    hi = lax.bitcast_convert_type(w & jnp.uint32(HI_MASK), F32)
    lo = lax.bitcast_convert_type(w << jnp.uint32(16), F32)
    return hi, lo


def _cparams(sem):
    return pltpu.CompilerParams(dimension_semantics=sem, vmem_limit_bytes=VMEM_LIMIT)


def _layer_norm(xf, g, b):
    mu = jnp.mean(xf, axis=-1, keepdims=True)
    xc = xf - mu
    var = jnp.mean(xc * xc, axis=-1, keepdims=True)
    return xc * lax.rsqrt(var + LN_EPS) * g + b


def _proj_kernel(x_ref, w_ref, o_ref, *, act):
    h = jnp.dot(x_ref[...], w_ref[...], preferred_element_type=F32)
    if act == "gelu":
        h = jax.nn.gelu(h, approximate=True)
    elif act == "sigmoid":
        h = jax.nn.sigmoid(h)
    o_ref[...] = h.astype(o_ref.dtype)


def _proj(x, w, col0, ncols, act, out_dtype):
    t, k = x.shape
    off = col0 // TN
    return pl.pallas_call(
        functools.partial(_proj_kernel, act=act),
        grid=(t // TM, ncols // TN),
        in_specs=[pl.BlockSpec((TM, k), lambda m, n: (m, 0)),
                  pl.BlockSpec((k, TN), lambda m, n: (0, n + off))],
        out_specs=pl.BlockSpec((TM, TN), lambda m, n: (m, n)),
        out_shape=jax.ShapeDtypeStruct((t, ncols), out_dtype),
        compiler_params=_cparams(("parallel", "arbitrary")),
        name="in_proj_" + act,
    )(x, w)


def _spatial_prompt_kernel(u_ref, gv_ref, ws_ref, bias_ref, g_ref, b_ref, o_ref):
    vn = _layer_norm(gv_ref[...].astype(F32), g_ref[...], b_ref[...])
    row = lax.broadcasted_iota(jnp.int32, (CHUNK, CHUNK), 0)
    col = lax.broadcasted_iota(jnp.int32, (CHUNK, CHUNK), 1)
    causal = col <= row
    for h in range(HEADS_A):
        sl = slice(h * HEAD_DIM_A, (h + 1) * HEAD_DIM_A)
        w = jnp.where(causal, ws_ref[h], 0.0).astype(BF16)
        s = jnp.dot(w, vn[:, sl].astype(BF16), preferred_element_type=F32) + bias_ref[:, sl]
        o_ref[:, sl] = (u_ref[:, sl].astype(F32) * s).astype(o_ref.dtype)


def _spatial_prompt(ug, w_s, bias_slab, ln_g, ln_b):
    n_chunks = T_PROMPT // CHUNK
    return pl.pallas_call(
        _spatial_prompt_kernel,
        grid=(n_chunks,),
        in_specs=[pl.BlockSpec((CHUNK, CH_A), lambda c: (c, 0)),
                  pl.BlockSpec((CHUNK, CH_A), lambda c: (c, 1)),
                  pl.BlockSpec((HEADS_A, CHUNK, CHUNK), lambda c: (0, 0, 0)),
                  pl.BlockSpec((CHUNK, CH_A), lambda c: (0, 0)),
                  pl.BlockSpec((1, CH_A), lambda c: (0, 0)),
                  pl.BlockSpec((1, CH_A), lambda c: (0, 0))],
        out_specs=pl.BlockSpec((CHUNK, CH_A), lambda c: (c, 0)),
        out_shape=jax.ShapeDtypeStruct((T_PROMPT, CH_A), BF16),
        compiler_params=_cparams(("parallel",)),
        name="spatial_prompt",
    )(ug, ug, w_s, bias_slab, ln_g, ln_b)


def _spatial_sample_kernel(u_ref, gv_ref, wexp_ref, bias_ref, g_ref, b_ref, o_ref, vn_ref):
    vn_ref[...] = _layer_norm(gv_ref[...].astype(F32), g_ref[...], b_ref[...])
    for t in range(DEC_SEQ):
        rows = slice(t * DEC_BATCH, (t + 1) * DEC_BATCH)
        s = jnp.broadcast_to(bias_ref[t:t + 1, :], (DEC_BATCH, CH_A))
        for j in range(t + 1):
            r = t * DEC_SEQ + j
            s = s + wexp_ref[r:r + 1, :] * vn_ref[j * DEC_BATCH:(j + 1) * DEC_BATCH, :]
        o_ref[rows, :] = (u_ref[rows, :].astype(F32) * s).astype(o_ref.dtype)


def _spatial_sample(ug, wexp, bias_slab, ln_g, ln_b):
    blk = T_PROMPT // T_SAMPLE
    return pl.pallas_call(
        _spatial_sample_kernel,
        grid=(1,),
        in_specs=[pl.BlockSpec((T_SAMPLE, CH_A), lambda i: (blk, 0)),
                  pl.BlockSpec((T_SAMPLE, CH_A), lambda i: (blk, 1)),
                  pl.BlockSpec((DEC_SEQ * DEC_SEQ, CH_A), lambda i: (0, 0)),
                  pl.BlockSpec((CHUNK, CH_A), lambda i: (0, 0)),
                  pl.BlockSpec((1, CH_A), lambda i: (0, 0)),
                  pl.BlockSpec((1, CH_A), lambda i: (0, 0))],
        out_specs=[pl.BlockSpec((T_SAMPLE, CH_A), lambda i: (0, 0)),
                   pl.BlockSpec((T_SAMPLE, CH_A), lambda i: (0, 0))],
        out_shape=[jax.ShapeDtypeStruct((T_SAMPLE, CH_A), BF16),
                   jax.ShapeDtypeStruct((T_SAMPLE, CH_A), F32)],
        compiler_params=_cparams(("arbitrary",)),
        name="spatial_sample",
    )(ug, ug, wexp, bias_slab, ln_g, ln_b)


def _pool_project(d_of_group, wp_ref, scale_ref, o_ref):
    for g in range(POOL_GROUPS):
        sl = slice(g * POOL_GC, (g + 1) * POOL_GC)
        y = jnp.dot(d_of_group(g).astype(BF16), wp_ref[g], preferred_element_type=F32)
        o_ref[:, sl] = (y * scale_ref[:, sl]).astype(o_ref.dtype)


def _pool_prompt_kernel(z_ref, halo_ref, wp_ref, scale_ref, o_ref, cat_ref):
    i = pl.program_id(1)
    halo = halo_ref[...]
    cat_ref[0:POOL_HALO, :] = jnp.where(i == 0, jnp.zeros_like(halo), halo)
    cat_ref[POOL_HALO:, :] = z_ref[...]
    pos = i * POOL_TM + lax.broadcasted_iota(jnp.int32, (POOL_TM, POOL_GC), 0)

    def d_of_group(g):
        win = POOL_WINDOWS[g]
        sl = slice(g * POOL_GC, (g + 1) * POOL_GC)
        acc = cat_ref[POOL_HALO:POOL_HALO + POOL_TM, sl]
        for k in range(1, win):
            acc = acc + cat_ref[POOL_HALO - k:POOL_HALO - k + POOL_TM, sl]
        cnt = jnp.minimum(pos + 1, win).astype(F32)
        return acc / cnt - z_ref[:, sl]

    _pool_project(d_of_group, wp_ref, scale_ref, o_ref)


def _pool_prompt(z, w_pool, pool_scale):
    tiles = SEQ // POOL_TM
    per_tile = POOL_TM // POOL_HALO
    return pl.pallas_call(
        _pool_prompt_kernel,
        grid=(BATCH, tiles),
        in_specs=[pl.BlockSpec((POOL_TM, CH_B), lambda b, i: (b * tiles + i, 0)),
                  pl.BlockSpec((POOL_HALO, CH_B),
                               lambda b, i: (jnp.maximum((b * tiles + i) * per_tile - 1, 0), 0)),
                  pl.BlockSpec((POOL_GROUPS, POOL_GC, POOL_GC), lambda b, i: (0, 0, 0)),
                  pl.BlockSpec((1, CH_B), lambda b, i: (0, 0))],
        out_specs=pl.BlockSpec((POOL_TM, CH_B), lambda b, i: (b * tiles + i, 0)),
        out_shape=jax.ShapeDtypeStruct((T_PROMPT, CH_B), BF16),
        scratch_shapes=[pltpu.VMEM((POOL_HALO + POOL_TM, CH_B), F32)],
        compiler_params=_cparams(("parallel", "arbitrary")),
        name="pool_prompt",
    )(z, z, w_pool, pool_scale)


def _pool_sample_kernel(z_ref, buf_ref, wp_ref, scale_ref, o_ref, d_ref):
    def cat_row(r, sl):
        if r < POOL_BUF:
            return buf_ref[r, :, sl]
        t = r - POOL_BUF
        return z_ref[t * DEC_BATCH:(t + 1) * DEC_BATCH, sl]

    for g in range(POOL_GROUPS):
        win = POOL_WINDOWS[g]
        sl = slice(g * POOL_GC, (g + 1) * POOL_GC)
        for t in range(DEC_SEQ):
            acc = cat_row(POOL_BUF + t, sl)
            for k in range(1, win):
                acc = acc + cat_row(POOL_BUF + t - k, sl)
            cnt = float(min(PAST_LEN + t + 1, win))
            d_ref[t * DEC_BATCH:(t + 1) * DEC_BATCH, sl] = acc / cnt - cat_row(POOL_BUF + t, sl)

    _pool_project(lambda g: d_ref[:, g * POOL_GC:(g + 1) * POOL_GC], wp_ref, scale_ref, o_ref)


def _pool_sample(z, buf_t, w_pool, pool_scale):
    blk = T_PROMPT // T_SAMPLE
    return pl.pallas_call(
        _pool_sample_kernel,
        grid=(1,),
        in_specs=[pl.BlockSpec((T_SAMPLE, CH_B), lambda i: (blk, 0)),
                  pl.BlockSpec((POOL_BUF, DEC_BATCH, CH_B), lambda i: (0, 0, 0)),
                  pl.BlockSpec((POOL_GROUPS, POOL_GC, POOL_GC), lambda i: (0, 0, 0)),
                  pl.BlockSpec((1, CH_B), lambda i: (0, 0))],
        out_specs=pl.BlockSpec((T_SAMPLE, CH_B), lambda i: (0, 0)),
        out_shape=jax.ShapeDtypeStruct((T_SAMPLE, CH_B), BF16),
        scratch_shapes=[pltpu.VMEM((T_SAMPLE, CH_B), F32)],
        compiler_params=_cparams(("arbitrary",)),
        name="pool_sample",
    )(z, buf_t, w_pool, pool_scale)


def _merge_kernel(oa_ref, ob_ref, wa_ref, wb_ref, ga_ref, gb_ref, o_ref):
    a = jnp.dot(oa_ref[...], wa_ref[...], preferred_element_type=F32)
    b = jnp.dot(ob_ref[...], wb_ref[...], preferred_element_type=F32)
    o_ref[...] = (ga_ref[...].astype(F32) * a + gb_ref[...].astype(F32) * b).astype(o_ref.dtype)


def _merge(o_a, o_b, w_up_a, w_up_b, gates):
    nb = D_MODEL // TN
    return pl.pallas_call(
        _merge_kernel,
        grid=(T_ALL // TM, nb),
        in_specs=[pl.BlockSpec((TM, CH_A), lambda m, n: (m, 0)),
                  pl.BlockSpec((TM, CH_B), lambda m, n: (m, 0)),
                  pl.BlockSpec((CH_A, TN), lambda m, n: (0, n)),
                  pl.BlockSpec((CH_B, TN), lambda m, n: (0, n)),
                  pl.BlockSpec((TM, TN), lambda m, n: (m, n)),
                  pl.BlockSpec((TM, TN), lambda m, n: (m, n + nb))],
        out_specs=pl.BlockSpec((TM, TN), lambda m, n: (m, n)),
        out_shape=jax.ShapeDtypeStruct((T_ALL, D_MODEL), BF16),
        compiler_params=_cparams(("parallel", "arbitrary")),
        name="merge",
    )(o_a, o_b, w_up_a, w_up_b, gates, gates)


def _out_proj_kernel(m_ref, w_ref, xp_ref, xs_ref, o_ref):
    x = jnp.where(pl.program_id(0) < T_PROMPT // TM_OUT, xp_ref[...], xs_ref[...])
    o_ref[...] = ALPHA * x + jnp.dot(m_ref[...], w_ref[...], preferred_element_type=F32)


def _out_proj(m, w_o, x_prompt, x_sample):
    n_prompt = T_PROMPT // TM_OUT
    return pl.pallas_call(
        _out_proj_kernel,
        grid=(T_ALL // TM_OUT, D_MODEL // TN),
        in_specs=[pl.BlockSpec((TM_OUT, D_MODEL), lambda m_, n: (m_, 0)),
                  pl.BlockSpec((D_MODEL, TN), lambda m_, n: (0, n)),
                  pl.BlockSpec((TM_OUT, TN), lambda m_, n: (jnp.minimum(m_, n_prompt - 1), n)),
                  pl.BlockSpec((TM_OUT, TN), lambda m_, n: (0, n))],
        out_specs=pl.BlockSpec((TM_OUT, TN), lambda m_, n: (m_, n)),
        out_shape=jax.ShapeDtypeStruct((T_ALL, D_MODEL), F32),
        compiler_params=_cparams(("parallel", "arbitrary")),
        name="out_proj",
    )(m, w_o, x_prompt, x_sample)


def _route_tile(scores, bias, carry):
    tm = scores.shape[0]
    lane = lax.broadcasted_iota(jnp.int32, (tm, N_EXPERTS), 1)
    per_group = N_EXPERTS // N_GROUPS
    neg = -jnp.inf
    biased = scores + bias

    def first_argmax(v):
        m = jnp.max(v, axis=-1, keepdims=True)
        return m, jnp.min(jnp.where(v == m, lane, N_EXPERTS), axis=-1, keepdims=True)

    in_group = [jnp.logical_and(lane >= g * per_group, lane < (g + 1) * per_group) for g in range(N_GROUPS)]
    gscore = []
    for g in range(N_GROUPS):
        v = jnp.where(in_group[g], biased, neg)
        m1, i1 = first_argmax(v)
        m2 = jnp.max(jnp.where(lane == i1, neg, v), axis=-1, keepdims=True)
        gscore.append(m1 + m2)
    group_kept = jnp.zeros((tm, N_EXPERTS), jnp.int32)
    for g in range(N_GROUPS):
        beaten_by = jnp.zeros((tm, 1), jnp.int32)
        for h in range(N_GROUPS):
            if h == g:
                continue
            wins = gscore[h] >= gscore[g] if h < g else gscore[h] > gscore[g]
            beaten_by = beaten_by + jnp.where(wins, 1, 0)
        kept = jnp.where(beaten_by < TOPK_GROUPS, 1, 0)
        group_kept = group_kept + jnp.where(in_group[g], kept, 0)
    masked = jnp.where(group_kept > 0, biased, neg)

    idx, gate = [], []
    sel_f = jnp.zeros((tm, N_EXPERTS), F32)
    for _ in range(TOP_K):
        _, i = first_argmax(masked)
        hit = lane == i
        idx.append(i)
        gate.append(jnp.sum(jnp.where(hit, scores, 0.0), axis=-1, keepdims=True))
        sel_f = sel_f + jnp.where(hit, 1.0, 0.0)
        masked = jnp.where(hit, neg, masked)
    total = gate[0]
    for k in range(1, TOP_K):
        total = total + gate[k]
    gate = [gk / total * ROUTE_SCALE for gk in gate]

    r_i = lax.broadcasted_iota(jnp.int32, (tm, tm), 0)
    c_i = lax.broadcasted_iota(jnp.int32, (tm, tm), 1)
    lower = jnp.where(c_i < r_i, 1.0, 0.0).astype(BF16)
    rank_full = carry + jnp.dot(lower, sel_f.astype(BF16), preferred_element_type=F32)
    rank = [jnp.sum(jnp.where(lane == i, rank_full, 0.0), axis=-1, keepdims=True) for i in idx]
    return idx, gate, rank, jnp.sum(sel_f, axis=0, keepdims=True)


def _columns_to_lanes(cols, dtype):
    tm = cols[0].shape[0]
    lane = lax.broadcasted_iota(jnp.int32, (tm, LANES), 1)
    out = jnp.zeros((tm, LANES), dtype)
    for k, c in enumerate(cols):
        out = jnp.where(lane == k, c.astype(dtype), out)
    return out


def _ln_router_kernel(r_ref, g_ref, b_ref, wr_ref, br_ref, x_ref, xb_ref, xpk_ref, e_ref, gw_ref, rk_ref, cnt_ref):
    @pl.when(pl.program_id(0) == 0)
    def _init():
        cnt_ref[...] = jnp.zeros_like(cnt_ref)

    x1 = _layer_norm(r_ref[...], g_ref[...], b_ref[...])
    x_ref[...] = x1
    xb = x1.astype(BF16)
    xb_ref[...] = xb
    xpk_ref[...] = _pack_pair(x1[:, :HALF], x1[:, HALF:])
    scores = jax.nn.sigmoid(jnp.dot(xb, wr_ref[...], preferred_element_type=F32))
    idx, gate, rank, counts = _route_tile(scores, br_ref[...], cnt_ref[...])
    e_ref[...] = _columns_to_lanes(idx, jnp.int32)
    gw_ref[...] = _columns_to_lanes(gate, F32)
    rk_ref[...] = _columns_to_lanes(rank, jnp.int32)
    cnt_ref[...] = cnt_ref[...] + counts


def _ln_router(r, g, b, w_router, b_router):
    row = lambda i: (i, 0)
    fixed = lambda i: (0, 0)
    return pl.pallas_call(
        _ln_router_kernel,
        grid=(T_ALL // TM_LN,),
        in_specs=[pl.BlockSpec((TM_LN, D_MODEL), row),
                  pl.BlockSpec((1, D_MODEL), fixed),
                  pl.BlockSpec((1, D_MODEL), fixed),
                  pl.BlockSpec((D_MODEL, N_EXPERTS), fixed),
                  pl.BlockSpec((1, N_EXPERTS), fixed)],
        out_specs=[pl.BlockSpec((TM_LN, D_MODEL), row),
                   pl.BlockSpec((TM_LN, D_MODEL), row),
                   pl.BlockSpec((TM_LN, HALF), row),
                   pl.BlockSpec((TM_LN, LANES), row),
                   pl.BlockSpec((TM_LN, LANES), row),
                   pl.BlockSpec((TM_LN, LANES), row),
                   pl.BlockSpec((1, N_EXPERTS), fixed)],
        out_shape=[jax.ShapeDtypeStruct((T_ALL, D_MODEL), F32),
                   jax.ShapeDtypeStruct((T_ALL, D_MODEL), BF16),
                   jax.ShapeDtypeStruct((T_ALL, HALF), jnp.uint32),
                   jax.ShapeDtypeStruct((T_ALL, LANES), jnp.int32),
                   jax.ShapeDtypeStruct((T_ALL, LANES), F32),
                   jax.ShapeDtypeStruct((T_ALL, LANES), jnp.int32),
                   jax.ShapeDtypeStruct((1, N_EXPERTS), F32)],
        compiler_params=_cparams(("arbitrary",)),
        name="ln_router",
    )(r, g, b, w_router, b_router)


def _ln_split_kernel(r_ref, g_ref, b_ref, op_ref, os_ref):
    i = pl.program_id(0)
    y = _layer_norm(r_ref[...], g_ref[...], b_ref[...])

    @pl.when(i < T_PROMPT // TM_LN)
    def _prompt():
        op_ref[...] = y

    @pl.when(i >= T_PROMPT // TM_LN)
    def _sample():
        os_ref[...] = y


def _ln_split(r, g, b):
    fixed = lambda i: (0, 0)
    n_prompt = T_PROMPT // TM_LN
    return pl.pallas_call(
        _ln_split_kernel,
        grid=(T_ALL // TM_LN,),
        in_specs=[pl.BlockSpec((TM_LN, D_MODEL), lambda i: (i, 0)),
                  pl.BlockSpec((1, D_MODEL), fixed),
                  pl.BlockSpec((1, D_MODEL), fixed)],
        out_specs=[pl.BlockSpec((TM_LN, D_MODEL), lambda i: (jnp.minimum(i, n_prompt - 1), 0)),
                   pl.BlockSpec((TM_LN, D_MODEL), lambda i: (jnp.maximum(i - n_prompt, 0), 0))],
        out_shape=[jax.ShapeDtypeStruct((T_PROMPT, D_MODEL), F32),
                   jax.ShapeDtypeStruct((T_SAMPLE, D_MODEL), F32)],
        compiler_params=_cparams(("arbitrary",)),
        name="ln_final",
    )(r, g, b)


def _moe_kernel(item_e, item_start, item_n, sdst, x_hbm, *refs):
    w1_refs = refs[:MOE_DMA_PARTS]
    w3_refs = refs[MOE_DMA_PARTS:2 * MOE_DMA_PARTS]
    w2_refs = refs[2 * MOE_DMA_PARTS:3 * MOE_DMA_PARTS]
    y_hbm, xslab, xb, w13b, w2b, hacc, yslab, gsem, ssem = refs[3 * MOE_DMA_PARTS:]
    _moe_body(item_start, item_n, sdst, x_hbm, w1_refs, w3_refs, w2_refs, y_hbm,
              xslab, xb, w13b, w2b, hacc, yslab, gsem, ssem)


def _moe_body(item_start, item_n, sdst, x_hbm, w1_refs, w3_refs, w2_refs, y_hbm,
              xslab, xb, w13b, w2b, hacc, yslab, gsem, ssem):
    w = pl.program_id(0)
    q = pl.program_id(1)
    n = item_n[w]
    start = item_start[w]
    slot = w % 2
    last_q = MOE_SPLIT - 1

    def x_copy(tok, r, slot_):
        return pltpu.make_async_copy(x_hbm.at[tok], xslab.at[slot_, pl.ds(r * PK_STRIDE, PK_ROWS), :],
                                     gsem.at[slot_])

    def y_copy(r, dst):
        return pltpu.make_async_copy(yslab.at[pl.ds(r * PK_STRIDE, PK_ROWS), :], y_hbm.at[dst], ssem)

    def for_each_row(count, row_fn):
        groups = count // MOE_DMA_UNROLL

        def group_body(g, c):
            for u in range(MOE_DMA_UNROLL):
                row_fn(g * MOE_DMA_UNROLL + u)
            return c

        def tail_body(r, c):
            row_fn(r)
            return c

        lax.fori_loop(0, groups, group_body, 0)
        lax.fori_loop(groups * MOE_DMA_UNROLL, count, tail_body, 0)

    def start_gather(item, slot_):
        base = item_start[item]
        for_each_row(item_n[item],
                     lambda r: x_copy(lax.shift_right_logical(sdst[base + r], TOP_K_SHIFT), r, slot_).start())

    def wait_gather(count, slot_):
        for_each_row(count, lambda r: x_copy(0, r, slot_).wait())

    def wait_scatter(count):
        for_each_row(count, lambda r: y_copy(r, 0).wait())

    @pl.when(jnp.logical_and(w == 0, q == 0))
    def _first():
        xslab[...] = jnp.zeros_like(xslab)
        yslab[...] = jnp.zeros_like(yslab)
        hacc[...] = jnp.zeros_like(hacc)
        start_gather(0, 0)

    @pl.when(jnp.logical_and(q == 0, n > 0))
    def _wait_rows():
        wait_gather(n, slot)

    @pl.when(jnp.logical_and(q == 1, w + 1 < MOE_ITEMS))
    def _prefetch_rows():
        start_gather(w + 1, 1 - slot)

    @pl.when(jnp.logical_and(q == last_q, w > 0))
    def _wait_prev_scatter():
        wait_scatter(item_n[w - 1])

    @pl.when(n > 0)
    def _compute():
        nb = (n + MOE_BUCKET - 1) // MOE_BUCKET
        for b in range(1, MOE_ROWS // MOE_BUCKET + 1):
            rows = b * MOE_BUCKET

            @pl.when(nb == b)
            def _bucket(rows=rows):
                @pl.when(q == 0)
                def _unpack():
                    for s in range(PK_ROWS):
                        hi, lo = _unpack_pair(xslab[slot, pl.ds(s, rows, stride=PK_STRIDE), :])
                        for col, val in ((s * LANES, hi), (HALF + s * LANES, lo)):
                            xb[col // MOE_KQ, 0:rows, col % MOE_KQ:col % MOE_KQ + LANES] = val.astype(BF16)

                part = None
                kp, dp = MOE_KQ // MOE_DMA_PARTS, MOE_DQ // MOE_DMA_PARTS
                for c in range(MOE_KQ // MOE_KSUB):
                    ks = slice(c * MOE_KSUB, (c + 1) * MOE_KSUB)
                    for j in range(c * MOE_KSUB // kp, (c + 1) * MOE_KSUB // kp):
                        w13b[j * kp:(j + 1) * kp, :D_EXPERT] = w1_refs[j][...].astype(BF16)
                        w13b[j * kp:(j + 1) * kp, D_EXPERT:] = w3_refs[j][...].astype(BF16)
                    d = jnp.dot(xb[q, 0:rows, ks], w13b[ks, :], preferred_element_type=F32)
                    part = d if part is None else part + d
                h = jnp.where(q == 0, part, hacc[0:rows, :] + part)
                hacc[0:rows, :] = h
                for j in range(MOE_DMA_PARTS):
                    w2b[pl.ds(pl.multiple_of(q * MOE_DQ + j * dp, dp), dp), :] = w2_refs[j][...].astype(BF16)

                @pl.when(q == last_q)
                def _down():
                    a = (jax.nn.silu(h[:, :D_EXPERT]) * h[:, D_EXPERT:]).astype(BF16)
                    per_chunk = MOE_NCHUNK // LANES
                    for c in range(HALF // MOE_NCHUNK):
                        lo_cols = slice(c * MOE_NCHUNK, (c + 1) * MOE_NCHUNK)
                        hi_cols = slice(HALF + c * MOE_NCHUNK, HALF + (c + 1) * MOE_NCHUNK)
                        pk = _pack_pair(jnp.dot(a, w2b[:, lo_cols], preferred_element_type=F32),
                                        jnp.dot(a, w2b[:, hi_cols], preferred_element_type=F32))
                        for s in range(per_chunk):
                            yslab[pl.ds(c * per_chunk + s, rows, stride=PK_STRIDE), :] = pk[:, s * LANES:(s + 1) * LANES]

    @pl.when(jnp.logical_and(q == last_q, n > 0))
    def _scatter():
        for_each_row(n, lambda r: y_copy(r, sdst[start + r]).start())

    @pl.when(jnp.logical_and(jnp.logical_and(q == last_q, w == MOE_ITEMS - 1), n > 0))
    def _drain():
        wait_scatter(n)


def _moe_routed(x_pk, w1, w3, w2, item_e, item_start, item_n, sdst):
    def weight_block(part):
        def index_map(w, q, item_e, item_start, item_n, sdst):
            q_eff = jnp.where(item_n[w] > 0, q, MOE_SPLIT - 1)
            return (item_e[w], q_eff * MOE_DMA_PARTS + part, 0)

        return index_map

    grid_spec = pltpu.PrefetchScalarGridSpec(
        num_scalar_prefetch=4,
        grid=(MOE_ITEMS, MOE_SPLIT),
        in_specs=[pl.BlockSpec(memory_space=pl.ANY)]
        + [pl.BlockSpec((None, MOE_KQ // MOE_DMA_PARTS, D_EXPERT), weight_block(j)) for j in range(MOE_DMA_PARTS)]
        + [pl.BlockSpec((None, MOE_KQ // MOE_DMA_PARTS, D_EXPERT), weight_block(j)) for j in range(MOE_DMA_PARTS)]
        + [pl.BlockSpec((None, MOE_DQ // MOE_DMA_PARTS, D_MODEL), weight_block(j)) for j in range(MOE_DMA_PARTS)],
        out_specs=pl.BlockSpec(memory_space=pl.ANY),
        scratch_shapes=[
            pltpu.VMEM((2, MOE_ROWS * PK_STRIDE, LANES), jnp.uint32),
            pltpu.VMEM((MOE_SPLIT, MOE_ROWS, MOE_KQ), BF16),
            pltpu.VMEM((MOE_KQ, 2 * D_EXPERT), BF16),
            pltpu.VMEM((D_EXPERT, D_MODEL), BF16),
            pltpu.VMEM((MOE_ROWS, 2 * D_EXPERT), F32),
            pltpu.VMEM((MOE_ROWS * PK_STRIDE, LANES), jnp.uint32),
            pltpu.SemaphoreType.DMA((2,)),
            pltpu.SemaphoreType.DMA(()),
        ],
    )
    return pl.pallas_call(
        _moe_kernel,
        grid_spec=grid_spec,
        out_shape=jax.ShapeDtypeStruct((N_ASSIGN, PK_ROWS, LANES), jnp.uint32),
        compiler_params=_cparams(("arbitrary", "arbitrary")),
        name="moe_routed",
    )(item_e, item_start, item_n, sdst, x_pk, *([w1] * MOE_DMA_PARTS + [w3] * MOE_DMA_PARTS + [w2] * MOE_DMA_PARTS))


def _shared_kernel(x_ref, xb_ref, w1_ref, w3_ref, w2_ref, o_ref):
    xb = xb_ref[...]
    a = jax.nn.silu(jnp.dot(xb, w1_ref[...], preferred_element_type=F32))
    a = (a * jnp.dot(xb, w3_ref[...], preferred_element_type=F32)).astype(BF16)
    o_ref[...] = ALPHA * x_ref[...] + jnp.dot(a, w2_ref[...], preferred_element_type=F32)


def _shared_expert(x, xb, ws1, ws3, ws2):
    fixed = lambda i: (0, 0)
    return pl.pallas_call(
        _shared_kernel,
        grid=(T_ALL // TM_SHARED,),
        in_specs=[pl.BlockSpec((TM_SHARED, D_MODEL), lambda i: (i, 0)),
                  pl.BlockSpec((TM_SHARED, D_MODEL), lambda i: (i, 0)),
                  pl.BlockSpec((D_MODEL, D_SHARED), fixed),
                  pl.BlockSpec((D_MODEL, D_SHARED), fixed),
                  pl.BlockSpec((D_SHARED, D_MODEL), fixed)],
        out_specs=pl.BlockSpec((TM_SHARED, D_MODEL), lambda i: (i, 0)),
        out_shape=jax.ShapeDtypeStruct((T_ALL, D_MODEL), F32),
        compiler_params=_cparams(("parallel",)),
        name="shared_expert",
    )(x, xb, ws1, ws3, ws2)


def _combine_kernel(y_ref, gw_ref, base_ref, g_ref, b_ref, x_ref, xb_ref, slab_in, slab_out):
    tm = base_ref.shape[0]
    for s in range(SLAB_ROWS):
        slab_in[pl.ds(s, tm, stride=SLAB_STRIDE), :] = base_ref[:, s * LANES:(s + 1) * LANES]
    for s in range(SLAB_ROWS, SLAB_STRIDE):
        slab_in[pl.ds(s, tm, stride=SLAB_STRIDE), :] = jnp.zeros((tm, LANES), F32)
    base = slab_in[...].reshape(tm, SLAB_STRIDE, LANES)
    lo_rows, hi_rows = slice(0, PK_ROWS), slice(PK_ROWS, SLAB_ROWS)
    acc_lo = base[:, lo_rows, :]
    acc_hi = base[:, hi_rows, :]
    for k in range(TOP_K):
        y_lo, y_hi = _unpack_pair(y_ref[:, k])
        gate = gw_ref[:, k:k + 1, :]
        acc_lo = acc_lo + y_lo * gate
        acc_hi = acc_hi + y_hi * gate

    def token_sum(a, b):
        return jnp.sum(jnp.sum(a, axis=2, keepdims=True) + jnp.sum(b, axis=2, keepdims=True), axis=1, keepdims=True)

    inv_d = 1.0 / D_MODEL
    mu = token_sum(acc_lo, acc_hi) * inv_d
    c_lo, c_hi = acc_lo - mu, acc_hi - mu
    inv_std = lax.rsqrt(token_sum(c_lo * c_lo, c_hi * c_hi) * inv_d + LN_EPS)
    out_lo = c_lo * inv_std * g_ref[:, lo_rows, :] + b_ref[:, lo_rows, :]
    out_hi = c_hi * inv_std * g_ref[:, hi_rows, :] + b_ref[:, hi_rows, :]
    pad = jnp.zeros((tm, SLAB_STRIDE - SLAB_ROWS, LANES), F32)
    slab_out[...] = jnp.concatenate([out_lo, out_hi, pad], axis=1).reshape(tm * SLAB_STRIDE, LANES)
    for s in range(SLAB_ROWS):
        cols = slab_out[pl.ds(s, tm, stride=SLAB_STRIDE), :]
        x_ref[:, s * LANES:(s + 1) * LANES] = cols
        xb_ref[:, s * LANES:(s + 1) * LANES] = cols.astype(BF16)


def _combine(y, gwb, base, g_slab, b_slab):
    row = lambda i: (i, 0)
    return pl.pallas_call(
        _combine_kernel,
        grid=(T_ALL // TM_COMBINE,),
        in_specs=[pl.BlockSpec((TM_COMBINE, TOP_K, PK_ROWS, LANES), lambda i: (i, 0, 0, 0)),
                  pl.BlockSpec((TM_COMBINE, TOP_K, LANES), lambda i: (i, 0, 0)),
                  pl.BlockSpec((TM_COMBINE, D_MODEL), row),
                  pl.BlockSpec((1, SLAB_ROWS, LANES), lambda i: (0, 0, 0)),
                  pl.BlockSpec((1, SLAB_ROWS, LANES), lambda i: (0, 0, 0))],
        out_specs=[pl.BlockSpec((TM_COMBINE, D_MODEL), row),
                   pl.BlockSpec((TM_COMBINE, D_MODEL), row)],
        out_shape=[jax.ShapeDtypeStruct((T_ALL, D_MODEL), F32),
                   jax.ShapeDtypeStruct((T_ALL, D_MODEL), BF16)],
        scratch_shapes=[pltpu.VMEM((TM_COMBINE * SLAB_STRIDE, LANES), F32),
                        pltpu.VMEM((TM_COMBINE * SLAB_STRIDE, LANES), F32)],
        compiler_params=_cparams(("parallel",)),
        name="moe_combine_ln",
    )(y, gwb, base, g_slab, b_slab)


def _ple_kernel(xb_ref, wg_ref, p_ref, wp_ref, x_ref, o_ref):
    gate = jax.nn.sigmoid(jnp.dot(xb_ref[...], wg_ref[...], preferred_element_type=F32))
    proj = jnp.dot(p_ref[...], wp_ref[...], preferred_element_type=F32)
    o_ref[...] = ALPHA * x_ref[...] + gate * proj


def _ple(xb, w_gate, p, w_proj, x):
    return pl.pallas_call(
        _ple_kernel,
        grid=(T_ALL // TM, D_MODEL // TN),
        in_specs=[pl.BlockSpec((TM, D_MODEL), lambda m, n: (m, 0)),
                  pl.BlockSpec((D_MODEL, TN), lambda m, n: (0, n)),
                  pl.BlockSpec((TM, PLE_DIM), lambda m, n: (m, 0)),
                  pl.BlockSpec((PLE_DIM, TN), lambda m, n: (0, n)),
                  pl.BlockSpec((TM, TN), lambda m, n: (m, n))],
        out_specs=pl.BlockSpec((TM, TN), lambda m, n: (m, n)),
        out_shape=jax.ShapeDtypeStruct((T_ALL, D_MODEL), F32),
        compiler_params=_cparams(("parallel", "arbitrary")),
        name="ple",
    )(xb, w_gate, p, w_proj, x)


def _sorted_pos_kernel(e_ref, rk_ref, start_ref, o_ref):
    tm = e_ref.shape[0]
    lane = lax.broadcasted_iota(jnp.int32, (tm, N_EXPERTS), 1)
    cols = []
    for k in range(TOP_K):
        seg = jnp.sum(jnp.where(lane == e_ref[:, k:k + 1], start_ref[...], 0.0), axis=-1, keepdims=True)
        cols.append(seg.astype(jnp.int32) + rk_ref[:, k:k + 1])
    o_ref[...] = _columns_to_lanes(cols, jnp.int32)


def _sorted_pos(e_pad, rank_pad, seg_start):
    row = lambda i: (i, 0)
    return pl.pallas_call(
        _sorted_pos_kernel,
        grid=(T_ALL // TM,),
        in_specs=[pl.BlockSpec((TM, LANES), row),
                  pl.BlockSpec((TM, LANES), row),
                  pl.BlockSpec((1, N_EXPERTS), lambda i: (0, 0))],
        out_specs=pl.BlockSpec((TM, LANES), row),
        out_shape=jax.ShapeDtypeStruct((T_ALL, LANES), jnp.int32),
        compiler_params=_cparams(("parallel",)),
        name="sorted_pos",
    )(e_pad, rank_pad, seg_start)


def _invert_kernel(pos_ref, o_ref):
    def body(g, c):
        for u in range(MOE_DMA_UNROLL):
            i = g * MOE_DMA_UNROLL + u
            o_ref[pos_ref[i]] = i
        return c

    lax.fori_loop(0, N_ASSIGN // MOE_DMA_UNROLL, body, 0)


def _invert_permutation(pos):
    return pl.pallas_call(
        _invert_kernel,
        grid_spec=pltpu.PrefetchScalarGridSpec(
            num_scalar_prefetch=1, grid=(1,), in_specs=[],
            out_specs=pl.BlockSpec(memory_space=pltpu.SMEM)),
        out_shape=jax.ShapeDtypeStruct((N_ASSIGN,), jnp.int32),
        compiler_params=_cparams(("arbitrary",)),
        name="invert_permutation",
    )(pos)


def _dispatch(e_pad, rank_pad, counts):
    sstart = jnp.cumsum(counts) - counts
    pos = _sorted_pos(e_pad, rank_pad, sstart.astype(F32).reshape(1, N_EXPERTS))
    sdst = _invert_permutation(pos[:, :TOP_K].reshape(N_ASSIGN))
    n_items = (counts + MOE_ROWS - 1) // MOE_ROWS
    item_end = jnp.cumsum(n_items)
    total = item_end[-1]
    w = jnp.arange(MOE_ITEMS, dtype=jnp.int32)
    e_of = jnp.minimum(jnp.searchsorted(item_end, w, side="right"), N_EXPERTS - 1).astype(jnp.int32)
    local = w - (item_end - n_items)[e_of]
    valid = w < total
    e_last = e_of[jnp.maximum(total - 1, 0)]
    item_e = jnp.where(valid, e_of, e_last)
    item_start = jnp.where(valid, sstart[e_of] + local * MOE_ROWS, 0)
    item_n = jnp.where(valid, jnp.clip(counts[e_of] - local * MOE_ROWS, 0, MOE_ROWS), 0)
    return item_e.astype(jnp.int32), item_start.astype(jnp.int32), item_n.astype(jnp.int32), sdst


def kernel(x_prompt, x_sample, state_pool, p_prompt, p_sample, w_in, ln_v_g, ln_v_b, w_s, b_s, w_pool, pool_scale, w_up_a, w_up_b, w_o, ln_g, ln_b, w_router, b_router, w1, w3, w2, ws1, ws3, ws2, w_ple_gate, w_ple_proj):
    def tokens(a_prompt, a_sample):
        d = a_prompt.shape[-1]
        return jnp.concatenate([a_prompt.reshape(T_PROMPT, d).astype(BF16),
                                a_sample.transpose(1, 0, 2).reshape(T_SAMPLE, d).astype(BF16)], axis=0)

    xb = tokens(x_prompt, x_sample)
    pb = tokens(p_prompt[0], p_sample[0])
    x_prompt_rows = x_prompt.reshape(T_PROMPT, D_MODEL)
    x_sample_rows = x_sample.transpose(1, 0, 2).reshape(T_SAMPLE, D_MODEL)
    row2d = lambda v: v.reshape(1, -1).astype(F32)

    w_in_b = w_in[0].astype(BF16)
    ug = _proj(xb, w_in_b, 0, 2 * CH_A, "gelu", BF16)
    z = _proj(xb, w_in_b, 2 * CH_A, CH_B, "none", F32)
    gates = _proj(xb, w_in_b, 2 * CH_A + CH_B, 2 * D_MODEL, "sigmoid", BF16)

    bias_slab = jnp.repeat(b_s[0].T, HEAD_DIM_A, axis=1).astype(F32)
    wexp = jnp.repeat(w_s[0][:, :DEC_SEQ, :DEC_SEQ].transpose(1, 2, 0).reshape(DEC_SEQ * DEC_SEQ, HEADS_A),
                      HEAD_DIM_A, axis=1).astype(F32)
    lvg, lvb = row2d(ln_v_g[0]), row2d(ln_v_b[0])
    oa_p = _spatial_prompt(ug, w_s[0], bias_slab, lvg, lvb)
    oa_s, vn_s = _spatial_sample(ug, wexp, bias_slab, lvg, lvb)
    o_a = jnp.concatenate([oa_p, oa_s], axis=0)

    w_pool_b = w_pool[0].astype(BF16)
    pscale = row2d(pool_scale[0])
    buf_t = state_pool[0].transpose(1, 0, 2)
    ob_p = _pool_prompt(z, w_pool_b, pscale)
    ob_s = _pool_sample(z, buf_t, w_pool_b, pscale)
    o_b = jnp.concatenate([ob_p, ob_s], axis=0)

    m = _merge(o_a, o_b, w_up_a[0].astype(BF16), w_up_b[0].astype(BF16), gates)
    r1 = _out_proj(m, w_o[0].astype(BF16), x_prompt_rows, x_sample_rows)
    x1, x1b, x1pk, e_pad, gw_pad, rank_pad, counts = _ln_router(
        r1, row2d(ln_g[0, 0]), row2d(ln_b[0, 0]), w_router[0].astype(BF16), row2d(b_router[0]))

    item_e, item_start, item_n, sdst = _dispatch(e_pad, rank_pad, counts[0].astype(jnp.int32))
    y = _moe_routed(x1pk.reshape(T_ALL, PK_ROWS, LANES), w1[0], w3[0], w2[0], item_e, item_start, item_n, sdst)
    base = _shared_expert(x1, x1b, ws1[0].astype(BF16), ws3[0].astype(BF16), ws2[0].astype(BF16))
    gwb = jnp.broadcast_to(gw_pad[:, :TOP_K].reshape(T_ALL, TOP_K, 1), (T_ALL, TOP_K, LANES))
    slab = lambda v: v.reshape(1, SLAB_ROWS, LANES).astype(F32)
    x2, x2b = _combine(y.reshape(T_ALL, TOP_K, PK_ROWS, LANES), gwb, base, slab(ln_g[0, 1]), slab(ln_b[0, 1]))

    r3 = _ple(x2b, w_ple_gate[0].astype(BF16), pb, w_ple_proj[0].astype(BF16), x2)
    y_p, y_s = _ln_split(r3, row2d(ln_g[0, 2]), row2d(ln_b[0, 2]))

    y_prompt = y_p.reshape(BATCH, SEQ, D_MODEL)
    y_sample = y_s.reshape(DEC_SEQ, DEC_BATCH, D_MODEL).transpose(1, 0, 2)
    z_s = z[T_PROMPT:].reshape(DEC_SEQ, DEC_BATCH, CH_B).transpose(1, 0, 2)
    new_pool_prompt = jnp.stack([z[(b + 1) * SEQ - POOL_BUF:(b + 1) * SEQ] for b in range(BATCH)])[None]
    new_pool_sample = jnp.concatenate([state_pool[0][:, DEC_SEQ:], z_s], axis=1)[None]
    new_chunk_v_sample = vn_s.reshape(DEC_SEQ, DEC_BATCH, CH_A).transpose(1, 0, 2)[None]
    return (y_prompt, y_sample, new_pool_prompt, new_pool_sample, new_chunk_v_sample)
```

```python
import functools

import jax
import jax.numpy as jnp
from jax import lax
from jax.experimental import pallas as pl
from jax.experimental.pallas import tpu as pltpu

F32 = jnp.float32
BF16 = jnp.bfloat16

D_MODEL = 4096
BATCH = 4
SEQ = 2048
DEC_BATCH = 128
DEC_SEQ = 4
PAST_LEN = 16384
CHUNK = 128
HEAD_DIM_A = 128
HEADS_A = 16
CH_A = 2048
POOL_WINDOWS = (2, 4, 8, 16)
POOL_GROUPS = 4
CH_B = 2048
POOL_GC = 512
POOL_BUF = 15
N_EXPERTS = 256
TOP_K = 8
N_GROUPS = 8
TOPK_GROUPS = 4
D_EXPERT = 512
D_SHARED = 512
ROUTE_SCALE = 2.5
PLE_DIM = 256
ALPHA = 2.0 ** 0.25
LN_EPS = 1e-5

T_PROMPT = BATCH * SEQ
T_SAMPLE = DEC_BATCH * DEC_SEQ
T_ALL = T_PROMPT + T_SAMPLE
N_ASSIGN = T_ALL * TOP_K

LANES = 128
SLAB_ROWS = D_MODEL // LANES
SLAB_STRIDE = 40
VMEM_LIMIT = 56 * 1024 * 1024

TM = 1088
TN = 512
TM_OUT = 512
TM_LN = 256
TM_SHARED = 256
TM_COMBINE = 64
POOL_TM = 512
POOL_HALO = 16
MOE_ROWS = 512
MOE_BUCKET = 64
MOE_SPLIT = 4
MOE_DQ = D_EXPERT // MOE_SPLIT
MOE_KQ = D_MODEL // MOE_SPLIT
MOE_KSUB = 512
MOE_DMA_PARTS = 1
MOE_ITEMS = N_EXPERTS + N_ASSIGN // MOE_ROWS
HALF = D_MODEL // 2
PK_ROWS = HALF // LANES
PK_STRIDE = 24
MOE_NCHUNK = 512
MOE_DMA_UNROLL = 8
MOE_WAIT_GROUP = 16
TOP_K_SHIFT = 3
HI_MASK = 0xFFFF0000


def _pack_pair(hi, lo):
    h = lax.bitcast_convert_type(hi.astype(BF16).astype(F32), jnp.uint32)
    l = lax.bitcast_convert_type(lo.astype(BF16).astype(F32), jnp.uint32)
    return h | (l >> jnp.uint32(16))


def _unpack_pair(w):
    hi = lax.bitcast_convert_type(w & jnp.uint32(HI_MASK), F32)
    lo = lax.bitcast_convert_type(w << jnp.uint32(16), F32)
    return hi, lo


def _cparams(sem):
    return pltpu.CompilerParams(dimension_semantics=sem, vmem_limit_bytes=VMEM_LIMIT)


def _layer_norm(xf, g, b):
    mu = jnp.mean(xf, axis=-1, keepdims=True)
    xc = xf - mu
    var = jnp.mean(xc * xc, axis=-1, keepdims=True)
    return xc * lax.rsqrt(var + LN_EPS) * g + b


def _proj_kernel(x_ref, w_ref, o_ref, *, act):
    h = jnp.dot(x_ref[...], w_ref[...], preferred_element_type=F32)
    if act == "gelu":
        h = jax.nn.gelu(h, approximate=True)
    elif act == "sigmoid":
        h = jax.nn.sigmoid(h)
    o_ref[...] = h.astype(o_ref.dtype)


def _proj(x, w, col0, ncols, act, out_dtype):
    t, k = x.shape
    off = col0 // TN
    return pl.pallas_call(
        functools.partial(_proj_kernel, act=act),
        grid=(t // TM, ncols // TN),
        in_specs=[pl.BlockSpec((TM, k), lambda m, n: (m, 0)),
                  pl.BlockSpec((k, TN), lambda m, n: (0, n + off))],
        out_specs=pl.BlockSpec((TM, TN), lambda m, n: (m, n)),
        out_shape=jax.ShapeDtypeStruct((t, ncols), out_dtype),
        compiler_params=_cparams(("parallel", "arbitrary")),
        name="in_proj_" + act,
    )(x, w)


def _spatial_prompt_kernel(u_ref, gv_ref, ws_ref, bias_ref, g_ref, b_ref, o_ref):
    vn = _layer_norm(gv_ref[...].astype(F32), g_ref[...], b_ref[...])
    row = lax.broadcasted_iota(jnp.int32, (CHUNK, CHUNK), 0)
    col = lax.broadcasted_iota(jnp.int32, (CHUNK, CHUNK), 1)
    causal = col <= row
    for h in range(HEADS_A):
        sl = slice(h * HEAD_DIM_A, (h + 1) * HEAD_DIM_A)
        w = jnp.where(causal, ws_ref[h], 0.0).astype(BF16)
        s = jnp.dot(w, vn[:, sl].astype(BF16), preferred_element_type=F32) + bias_ref[:, sl]
        o_ref[:, sl] = (u_ref[:, sl].astype(F32) * s).astype(o_ref.dtype)


def _spatial_prompt(ug, w_s, bias_slab, ln_g, ln_b):
    n_chunks = T_PROMPT // CHUNK
    return pl.pallas_call(
        _spatial_prompt_kernel,
        grid=(n_chunks,),
        in_specs=[pl.BlockSpec((CHUNK, CH_A), lambda c: (c, 0)),
                  pl.BlockSpec((CHUNK, CH_A), lambda c: (c, 1)),
                  pl.BlockSpec((HEADS_A, CHUNK, CHUNK), lambda c: (0, 0, 0)),
                  pl.BlockSpec((CHUNK, CH_A), lambda c: (0, 0)),
                  pl.BlockSpec((1, CH_A), lambda c: (0, 0)),
                  pl.BlockSpec((1, CH_A), lambda c: (0, 0))],
        out_specs=pl.BlockSpec((CHUNK, CH_A), lambda c: (c, 0)),
        out_shape=jax.ShapeDtypeStruct((T_PROMPT, CH_A), BF16),
        compiler_params=_cparams(("parallel",)),
        name="spatial_prompt",
    )(ug, ug, w_s, bias_slab, ln_g, ln_b)


def _spatial_sample_kernel(u_ref, gv_ref, wexp_ref, bias_ref, g_ref, b_ref, o_ref, vn_ref):
    vn_ref[...] = _layer_norm(gv_ref[...].astype(F32), g_ref[...], b_ref[...])
    for t in range(DEC_SEQ):
        rows = slice(t * DEC_BATCH, (t + 1) * DEC_BATCH)
        s = jnp.broadcast_to(bias_ref[t:t + 1, :], (DEC_BATCH, CH_A))
        for j in range(t + 1):
            r = t * DEC_SEQ + j
            s = s + wexp_ref[r:r + 1, :] * vn_ref[j * DEC_BATCH:(j + 1) * DEC_BATCH, :]
        o_ref[rows, :] = (u_ref[rows, :].astype(F32) * s).astype(o_ref.dtype)


def _spatial_sample(ug, wexp, bias_slab, ln_g, ln_b):
    blk = T_PROMPT // T_SAMPLE
    return pl.pallas_call(
        _spatial_sample_kernel,
        grid=(1,),
        in_specs=[pl.BlockSpec((T_SAMPLE, CH_A), lambda i: (blk, 0)),
                  pl.BlockSpec((T_SAMPLE, CH_A), lambda i: (blk, 1)),
                  pl.BlockSpec((DEC_SEQ * DEC_SEQ, CH_A), lambda i: (0, 0)),
                  pl.BlockSpec((CHUNK, CH_A), lambda i: (0, 0)),
                  pl.BlockSpec((1, CH_A), lambda i: (0, 0)),
                  pl.BlockSpec((1, CH_A), lambda i: (0, 0))],
        out_specs=[pl.BlockSpec((T_SAMPLE, CH_A), lambda i: (0, 0)),
                   pl.BlockSpec((T_SAMPLE, CH_A), lambda i: (0, 0))],
        out_shape=[jax.ShapeDtypeStruct((T_SAMPLE, CH_A), BF16),
                   jax.ShapeDtypeStruct((T_SAMPLE, CH_A), F32)],
        compiler_params=_cparams(("arbitrary",)),
        name="spatial_sample",
    )(ug, ug, wexp, bias_slab, ln_g, ln_b)


def _pool_project(d_of_group, wp_ref, scale_ref, o_ref):
    for g in range(POOL_GROUPS):
        sl = slice(g * POOL_GC, (g + 1) * POOL_GC)
        y = jnp.dot(d_of_group(g).astype(BF16), wp_ref[g], preferred_element_type=F32)
        o_ref[:, sl] = (y * scale_ref[:, sl]).astype(o_ref.dtype)


def _pool_prompt_kernel(z_ref, halo_ref, wp_ref, scale_ref, o_ref, cat_ref):
    i = pl.program_id(1)
    halo = halo_ref[...]
    cat_ref[0:POOL_HALO, :] = jnp.where(i == 0, jnp.zeros_like(halo), halo)
    cat_ref[POOL_HALO:, :] = z_ref[...]
    pos = i * POOL_TM + lax.broadcasted_iota(jnp.int32, (POOL_TM, POOL_GC), 0)

    def d_of_group(g):
        win = POOL_WINDOWS[g]
        sl = slice(g * POOL_GC, (g + 1) * POOL_GC)
        acc = cat_ref[POOL_HALO:POOL_HALO + POOL_TM, sl]
        for k in range(1, win):
            acc = acc + cat_ref[POOL_HALO - k:POOL_HALO - k + POOL_TM, sl]
        cnt = jnp.minimum(pos + 1, win).astype(F32)
        return acc / cnt - z_ref[:, sl]

    _pool_project(d_of_group, wp_ref, scale_ref, o_ref)


def _pool_prompt(z, w_pool, pool_scale):
    tiles = SEQ // POOL_TM
    per_tile = POOL_TM // POOL_HALO
    return pl.pallas_call(
        _pool_prompt_kernel,
        grid=(BATCH, tiles),
        in_specs=[pl.BlockSpec((POOL_TM, CH_B), lambda b, i: (b * tiles + i, 0)),
                  pl.BlockSpec((POOL_HALO, CH_B),
                               lambda b, i: (jnp.maximum((b * tiles + i) * per_tile - 1, 0), 0)),
                  pl.BlockSpec((POOL_GROUPS, POOL_GC, POOL_GC), lambda b, i: (0, 0, 0)),
                  pl.BlockSpec((1, CH_B), lambda b, i: (0, 0))],
        out_specs=pl.BlockSpec((POOL_TM, CH_B), lambda b, i: (b * tiles + i, 0)),
        out_shape=jax.ShapeDtypeStruct((T_PROMPT, CH_B), BF16),
        scratch_shapes=[pltpu.VMEM((POOL_HALO + POOL_TM, CH_B), F32)],
        compiler_params=_cparams(("parallel", "arbitrary")),
        name="pool_prompt",
    )(z, z, w_pool, pool_scale)


def _pool_sample_kernel(z_ref, buf_ref, wp_ref, scale_ref, o_ref, d_ref):
    def cat_row(r, sl):
        if r < POOL_BUF:
            return buf_ref[r, :, sl]
        t = r - POOL_BUF
        return z_ref[t * DEC_BATCH:(t + 1) * DEC_BATCH, sl]

    for g in range(POOL_GROUPS):
        win = POOL_WINDOWS[g]
        sl = slice(g * POOL_GC, (g + 1) * POOL_GC)
        for t in range(DEC_SEQ):
            acc = cat_row(POOL_BUF + t, sl)
            for k in range(1, win):
                acc = acc + cat_row(POOL_BUF + t - k, sl)
            cnt = float(min(PAST_LEN + t + 1, win))
            d_ref[t * DEC_BATCH:(t + 1) * DEC_BATCH, sl] = acc / cnt - cat_row(POOL_BUF + t, sl)

    _pool_project(lambda g: d_ref[:, g * POOL_GC:(g + 1) * POOL_GC], wp_ref, scale_ref, o_ref)


def _pool_sample(z, buf_t, w_pool, pool_scale):
    blk = T_PROMPT // T_SAMPLE
    return pl.pallas_call(
        _pool_sample_kernel,
        grid=(1,),
        in_specs=[pl.BlockSpec((T_SAMPLE, CH_B), lambda i: (blk, 0)),
                  pl.BlockSpec((POOL_BUF, DEC_BATCH, CH_B), lambda i: (0, 0, 0)),
                  pl.BlockSpec((POOL_GROUPS, POOL_GC, POOL_GC), lambda i: (0, 0, 0)),
                  pl.BlockSpec((1, CH_B), lambda i: (0, 0))],
        out_specs=pl.BlockSpec((T_SAMPLE, CH_B), lambda i: (0, 0)),
        out_shape=jax.ShapeDtypeStruct((T_SAMPLE, CH_B), BF16),
        scratch_shapes=[pltpu.VMEM((T_SAMPLE, CH_B), F32)],
        compiler_params=_cparams(("arbitrary",)),
        name="pool_sample",
    )(z, buf_t, w_pool, pool_scale)


def _merge_kernel(oa_ref, ob_ref, wa_ref, wb_ref, ga_ref, gb_ref, o_ref):
    a = jnp.dot(oa_ref[...], wa_ref[...], preferred_element_type=F32)
    b = jnp.dot(ob_ref[...], wb_ref[...], preferred_element_type=F32)
    o_ref[...] = (ga_ref[...].astype(F32) * a + gb_ref[...].astype(F32) * b).astype(o_ref.dtype)


def _merge(o_a, o_b, w_up_a, w_up_b, gates):
    nb = D_MODEL // TN
    return pl.pallas_call(
        _merge_kernel,
        grid=(T_ALL // TM, nb),
        in_specs=[pl.BlockSpec((TM, CH_A), lambda m, n: (m, 0)),
                  pl.BlockSpec((TM, CH_B), lambda m, n: (m, 0)),
                  pl.BlockSpec((CH_A, TN), lambda m, n: (0, n)),
                  pl.BlockSpec((CH_B, TN), lambda m, n: (0, n)),
                  pl.BlockSpec((TM, TN), lambda m, n: (m, n)),
                  pl.BlockSpec((TM, TN), lambda m, n: (m, n + nb))],
        out_specs=pl.BlockSpec((TM, TN), lambda m, n: (m, n)),
        out_shape=jax.ShapeDtypeStruct((T_ALL, D_MODEL), BF16),
        compiler_params=_cparams(("parallel", "arbitrary")),
        name="merge",
    )(o_a, o_b, w_up_a, w_up_b, gates, gates)


def _out_proj_kernel(m_ref, w_ref, xp_ref, xs_ref, o_ref):
    x = jnp.where(pl.program_id(0) < T_PROMPT // TM_OUT, xp_ref[...], xs_ref[...])
    o_ref[...] = ALPHA * x + jnp.dot(m_ref[...], w_ref[...], preferred_element_type=F32)


def _out_proj(m, w_o, x_prompt, x_sample):
    n_prompt = T_PROMPT // TM_OUT
    return pl.pallas_call(
        _out_proj_kernel,
        grid=(T_ALL // TM_OUT, D_MODEL // TN),
        in_specs=[pl.BlockSpec((TM_OUT, D_MODEL), lambda m_, n: (m_, 0)),
                  pl.BlockSpec((D_MODEL, TN), lambda m_, n: (0, n)),
                  pl.BlockSpec((TM_OUT, TN), lambda m_, n: (jnp.minimum(m_, n_prompt - 1), n)),
                  pl.BlockSpec((TM_OUT, TN), lambda m_, n: (0, n))],
        out_specs=pl.BlockSpec((TM_OUT, TN), lambda m_, n: (m_, n)),
        out_shape=jax.ShapeDtypeStruct((T_ALL, D_MODEL), F32),
        compiler_params=_cparams(("parallel", "arbitrary")),
        name="out_proj",
    )(m, w_o, x_prompt, x_sample)


def _route_tile(scores, bias, carry):
    tm = scores.shape[0]
    lane = lax.broadcasted_iota(jnp.int32, (tm, N_EXPERTS), 1)
    per_group = N_EXPERTS // N_GROUPS
    neg = -jnp.inf
    biased = scores + bias

    def first_argmax(v):
        m = jnp.max(v, axis=-1, keepdims=True)
        return m, jnp.min(jnp.where(v == m, lane, N_EXPERTS), axis=-1, keepdims=True)

    in_group = [jnp.logical_and(lane >= g * per_group, lane < (g + 1) * per_group) for g in range(N_GROUPS)]
    gscore = []
    for g in range(N_GROUPS):
        v = jnp.where(in_group[g], biased, neg)
        m1, i1 = first_argmax(v)
        m2 = jnp.max(jnp.where(lane == i1, neg, v), axis=-1, keepdims=True)
        gscore.append(m1 + m2)
    group_kept = jnp.zeros((tm, N_EXPERTS), jnp.int32)
    for g in range(N_GROUPS):
        beaten_by = jnp.zeros((tm, 1), jnp.int32)
        for h in range(N_GROUPS):
            if h == g:
                continue
            wins = gscore[h] >= gscore[g] if h < g else gscore[h] > gscore[g]
            beaten_by = beaten_by + jnp.where(wins, 1, 0)
        kept = jnp.where(beaten_by < TOPK_GROUPS, 1, 0)
        group_kept = group_kept + jnp.where(in_group[g], kept, 0)
    masked = jnp.where(group_kept > 0, biased, neg)

    idx, gate = [], []
    sel_f = jnp.zeros((tm, N_EXPERTS), F32)
    for _ in range(TOP_K):
        _, i = first_argmax(masked)
        hit = lane == i
        idx.append(i)
        gate.append(jnp.sum(jnp.where(hit, scores, 0.0), axis=-1, keepdims=True))
        sel_f = sel_f + jnp.where(hit, 1.0, 0.0)
        masked = jnp.where(hit, neg, masked)
    total = gate[0]
    for k in range(1, TOP_K):
        total = total + gate[k]
    gate = [gk / total * ROUTE_SCALE for gk in gate]

    r_i = lax.broadcasted_iota(jnp.int32, (tm, tm), 0)
    c_i = lax.broadcasted_iota(jnp.int32, (tm, tm), 1)
    lower = jnp.where(c_i < r_i, 1.0, 0.0).astype(BF16)
    rank_full = carry + jnp.dot(lower, sel_f.astype(BF16), preferred_element_type=F32)
    rank = [jnp.sum(jnp.where(lane == i, rank_full, 0.0), axis=-1, keepdims=True) for i in idx]
    return idx, gate, rank, jnp.sum(sel_f, axis=0, keepdims=True)


def _columns_to_lanes(cols, dtype):
    tm = cols[0].shape[0]
    lane = lax.broadcasted_iota(jnp.int32, (tm, LANES), 1)
    out = jnp.zeros((tm, LANES), dtype)
    for k, c in enumerate(cols):
        out = jnp.where(lane == k, c.astype(dtype), out)
    return out


def _ln_router_kernel(r_ref, g_ref, b_ref, wr_ref, br_ref, x_ref, xb_ref, xpk_ref, e_ref, gw_ref, rk_ref, cnt_ref):
    @pl.when(pl.program_id(0) == 0)
    def _init():
        cnt_ref[...] = jnp.zeros_like(cnt_ref)

    x1 = _layer_norm(r_ref[...], g_ref[...], b_ref[...])
    x_ref[...] = x1
    xb = x1.astype(BF16)
    xb_ref[...] = xb
    xpk_ref[...] = _pack_pair(x1[:, :HALF], x1[:, HALF:])
    scores = jax.nn.sigmoid(jnp.dot(xb, wr_ref[...], preferred_element_type=F32))
    idx, gate, rank, counts = _route_tile(scores, br_ref[...], cnt_ref[...])
    e_ref[...] = _columns_to_lanes(idx, jnp.int32)
    gw_ref[...] = _columns_to_lanes(gate, F32)
    rk_ref[...] = _columns_to_lanes(rank, jnp.int32)
    cnt_ref[...] = cnt_ref[...] + counts


def _ln_router(r, g, b, w_router, b_router):
    row = lambda i: (i, 0)
    fixed = lambda i: (0, 0)
    return pl.pallas_call(
        _ln_router_kernel,
        grid=(T_ALL // TM_LN,),
        in_specs=[pl.BlockSpec((TM_LN, D_MODEL), row),
                  pl.BlockSpec((1, D_MODEL), fixed),
                  pl.BlockSpec((1, D_MODEL), fixed),
                  pl.BlockSpec((D_MODEL, N_EXPERTS), fixed),
                  pl.BlockSpec((1, N_EXPERTS), fixed)],
        out_specs=[pl.BlockSpec((TM_LN, D_MODEL), row),
                   pl.BlockSpec((TM_LN, D_MODEL), row),
                   pl.BlockSpec((TM_LN, HALF), row),
                   pl.BlockSpec((TM_LN, LANES), row),
                   pl.BlockSpec((TM_LN, LANES), row),
                   pl.BlockSpec((TM_LN, LANES), row),
                   pl.BlockSpec((1, N_EXPERTS), fixed)],
        out_shape=[jax.ShapeDtypeStruct((T_ALL, D_MODEL), F32),
                   jax.ShapeDtypeStruct((T_ALL, D_MODEL), BF16),
                   jax.ShapeDtypeStruct((T_ALL, HALF), jnp.uint32),
                   jax.ShapeDtypeStruct((T_ALL, LANES), jnp.int32),
                   jax.ShapeDtypeStruct((T_ALL, LANES), F32),
                   jax.ShapeDtypeStruct((T_ALL, LANES), jnp.int32),
                   jax.ShapeDtypeStruct((1, N_EXPERTS), F32)],
        compiler_params=_cparams(("arbitrary",)),
        name="ln_router",
    )(r, g, b, w_router, b_router)


def _ln_split_kernel(r_ref, g_ref, b_ref, op_ref, os_ref):
    i = pl.program_id(0)
    y = _layer_norm(r_ref[...], g_ref[...], b_ref[...])

    @pl.when(i < T_PROMPT // TM_LN)
    def _prompt():
        op_ref[...] = y

    @pl.when(i >= T_PROMPT // TM_LN)
    def _sample():
        os_ref[...] = y


def _ln_split(r, g, b):
    fixed = lambda i: (0, 0)
    n_prompt = T_PROMPT // TM_LN
    return pl.pallas_call(
        _ln_split_kernel,
        grid=(T_ALL // TM_LN,),
        in_specs=[pl.BlockSpec((TM_LN, D_MODEL), lambda i: (i, 0)),
                  pl.BlockSpec((1, D_MODEL), fixed),
                  pl.BlockSpec((1, D_MODEL), fixed)],
        out_specs=[pl.BlockSpec((TM_LN, D_MODEL), lambda i: (jnp.minimum(i, n_prompt - 1), 0)),
                   pl.BlockSpec((TM_LN, D_MODEL), lambda i: (jnp.maximum(i - n_prompt, 0), 0))],
        out_shape=[jax.ShapeDtypeStruct((T_PROMPT, D_MODEL), F32),
                   jax.ShapeDtypeStruct((T_SAMPLE, D_MODEL), F32)],
        compiler_params=_cparams(("arbitrary",)),
        name="ln_final",
    )(r, g, b)


def _moe_kernel(item_e, item_start, item_n, sdst, x_hbm, *refs):
    w1_refs = refs[:MOE_DMA_PARTS]
    w3_refs = refs[MOE_DMA_PARTS:2 * MOE_DMA_PARTS]
    w2_refs = refs[2 * MOE_DMA_PARTS:3 * MOE_DMA_PARTS]
    y_hbm, xslab, xb, w13b, w2b, hacc, yslab, gsem, ssem = refs[3 * MOE_DMA_PARTS:]
    _moe_body(item_start, item_n, sdst, x_hbm, w1_refs, w3_refs, w2_refs, y_hbm,
              xslab, xb, w13b, w2b, hacc, yslab, gsem, ssem)


def _moe_body(item_start, item_n, sdst, x_hbm, w1_refs, w3_refs, w2_refs, y_hbm,
              xslab, xb, w13b, w2b, hacc, yslab, gsem, ssem):
    w = pl.program_id(0)
    q = pl.program_id(1)
    n = item_n[w]
    start = item_start[w]
    slot = w % 2
    last_q = MOE_SPLIT - 1

    def x_copy(tok, r, slot_):
        return pltpu.make_async_copy(x_hbm.at[pl.ds(tok * PK_ROWS, PK_ROWS), :],
                                     xslab.at[slot_, pl.ds(r * PK_STRIDE, PK_ROWS), :],
                                     gsem.at[slot_])

    def y_copy(r, dst):
        return pltpu.make_async_copy(yslab.at[pl.ds(r * PK_STRIDE, PK_ROWS), :],
                                     y_hbm.at[pl.ds(dst * PK_ROWS, PK_ROWS), :], ssem)

    def for_each_row(count, row_fn):
        groups = count // MOE_DMA_UNROLL

        def group_body(g, c):
            for u in range(MOE_DMA_UNROLL):
                row_fn(g * MOE_DMA_UNROLL + u)
            return c

        def tail_body(r, c):
            row_fn(r)
            return c

        lax.fori_loop(0, groups, group_body, 0)
        lax.fori_loop(groups * MOE_DMA_UNROLL, count, tail_body, 0)

    def start_gather(item, slot_):
        base = item_start[item]
        for_each_row(item_n[item],
                     lambda r: x_copy(lax.shift_right_logical(sdst[base + r], TOP_K_SHIFT), r, slot_).start())

    def wait_rows(count, group_wait, row_wait):
        groups = count // MOE_WAIT_GROUP

        def group_body(g, c):
            group_wait()
            return c

        def tail_body(r, c):
            row_wait(r)
            return c

        lax.fori_loop(0, groups, group_body, 0)
        lax.fori_loop(groups * MOE_WAIT_GROUP, count, tail_body, 0)

    group_rows = MOE_WAIT_GROUP * PK_ROWS

    def wait_gather(count, slot_):
        wait_rows(count,
                  lambda: pltpu.make_async_copy(x_hbm.at[pl.ds(0, group_rows), :],
                                                xslab.at[slot_, pl.ds(0, group_rows), :], gsem.at[slot_]).wait(),
                  lambda r: x_copy(0, r, slot_).wait())

    def wait_scatter(count):
        wait_rows(count,
                  lambda: pltpu.make_async_copy(yslab.at[pl.ds(0, group_rows), :],
                                                y_hbm.at[pl.ds(0, group_rows), :], ssem).wait(),
                  lambda r: y_copy(r, 0).wait())

    @pl.when(jnp.logical_and(w == 0, q == 0))
    def _first():
        xslab[...] = jnp.zeros_like(xslab)
        yslab[...] = jnp.zeros_like(yslab)
        hacc[...] = jnp.zeros_like(hacc)
        start_gather(0, 0)

    @pl.when(jnp.logical_and(q == 0, n > 0))
    def _wait_rows():
        wait_gather(n, slot)

    @pl.when(jnp.logical_and(q == 1, w + 1 < MOE_ITEMS))
    def _prefetch_rows():
        start_gather(w + 1, 1 - slot)

    @pl.when(jnp.logical_and(q == last_q, w > 0))
    def _wait_prev_scatter():
        wait_scatter(item_n[w - 1])

    @pl.when(n > 0)
    def _compute():
        nb = (n + MOE_BUCKET - 1) // MOE_BUCKET
        for b in range(1, MOE_ROWS // MOE_BUCKET + 1):
            rows = b * MOE_BUCKET

            @pl.when(nb == b)
            def _bucket(rows=rows):
                @pl.when(q == 0)
                def _unpack():
                    for s in range(PK_ROWS):
                        hi, lo = _unpack_pair(xslab[slot, pl.ds(s, rows, stride=PK_STRIDE), :])
                        for col, val in ((s * LANES, hi), (HALF + s * LANES, lo)):
                            xb[col // MOE_KQ, 0:rows, col % MOE_KQ:col % MOE_KQ + LANES] = val.astype(BF16)

                part = None
                kp, dp = MOE_KQ // MOE_DMA_PARTS, MOE_DQ // MOE_DMA_PARTS
                piece = min(kp, MOE_KSUB)
                for c in range(MOE_KQ // MOE_KSUB):
                    ks = slice(c * MOE_KSUB, (c + 1) * MOE_KSUB)
                    for r0 in range(c * MOE_KSUB, (c + 1) * MOE_KSUB, piece):
                        j, off = r0 // kp, r0 % kp
                        w13b[r0:r0 + piece, :D_EXPERT] = w1_refs[j][off:off + piece, :].astype(BF16)
                        w13b[r0:r0 + piece, D_EXPERT:] = w3_refs[j][off:off + piece, :].astype(BF16)
                    d = jnp.dot(xb[q, 0:rows, ks], w13b[ks, :], preferred_element_type=F32)
                    part = d if part is None else part + d
                h = jnp.where(q == 0, part, hacc[0:rows, :] + part)
                hacc[0:rows, :] = h
                for j in range(MOE_DMA_PARTS):
                    w2b[pl.ds(pl.multiple_of(q * MOE_DQ + j * dp, dp), dp), :] = w2_refs[j][...].astype(BF16)

                @pl.when(q == last_q)
                def _down():
                    a = (jax.nn.silu(h[:, :D_EXPERT]) * h[:, D_EXPERT:]).astype(BF16)
                    per_chunk = MOE_NCHUNK // LANES
                    for c in range(HALF // MOE_NCHUNK):
                        lo_cols = slice(c * MOE_NCHUNK, (c + 1) * MOE_NCHUNK)
                        hi_cols = slice(HALF + c * MOE_NCHUNK, HALF + (c + 1) * MOE_NCHUNK)
                        pk = _pack_pair(jnp.dot(a, w2b[:, lo_cols], preferred_element_type=F32),
                                        jnp.dot(a, w2b[:, hi_cols], preferred_element_type=F32))
                        for s in range(per_chunk):
                            yslab[pl.ds(c * per_chunk + s, rows, stride=PK_STRIDE), :] = pk[:, s * LANES:(s + 1) * LANES]

    @pl.when(jnp.logical_and(q == last_q, n > 0))
    def _scatter():
        for_each_row(n, lambda r: y_copy(r, sdst[start + r]).start())

    @pl.when(jnp.logical_and(jnp.logical_and(q == last_q, w == MOE_ITEMS - 1), n > 0))
    def _drain():
        wait_scatter(n)


def _moe_routed(x_pk, w1, w3, w2, item_e, item_start, item_n, sdst):
    def weight_block(part):
        def index_map(w, q, item_e, item_start, item_n, sdst):
            q_eff = jnp.where(item_n[w] > 0, q, MOE_SPLIT - 1)
            return (item_e[w], q_eff * MOE_DMA_PARTS + part, 0)

        return index_map

    grid_spec = pltpu.PrefetchScalarGridSpec(
        num_scalar_prefetch=4,
        grid=(MOE_ITEMS, MOE_SPLIT),
        in_specs=[pl.BlockSpec(memory_space=pl.ANY)]
        + [pl.BlockSpec((None, MOE_KQ // MOE_DMA_PARTS, D_EXPERT), weight_block(j)) for j in range(MOE_DMA_PARTS)]
        + [pl.BlockSpec((None, MOE_KQ // MOE_DMA_PARTS, D_EXPERT), weight_block(j)) for j in range(MOE_DMA_PARTS)]
        + [pl.BlockSpec((None, MOE_DQ // MOE_DMA_PARTS, D_MODEL), weight_block(j)) for j in range(MOE_DMA_PARTS)],
        out_specs=pl.BlockSpec(memory_space=pl.ANY),
        scratch_shapes=[
            pltpu.VMEM((2, MOE_ROWS * PK_STRIDE, LANES), jnp.uint32),
            pltpu.VMEM((MOE_SPLIT, MOE_ROWS, MOE_KQ), BF16),
            pltpu.VMEM((MOE_KQ, 2 * D_EXPERT), BF16),
            pltpu.VMEM((D_EXPERT, D_MODEL), BF16),
            pltpu.VMEM((MOE_ROWS, 2 * D_EXPERT), F32),
            pltpu.VMEM((MOE_ROWS * PK_STRIDE, LANES), jnp.uint32),
            pltpu.SemaphoreType.DMA((2,)),
            pltpu.SemaphoreType.DMA(()),
        ],
    )
    return pl.pallas_call(
        _moe_kernel,
        grid_spec=grid_spec,
        out_shape=jax.ShapeDtypeStruct((N_ASSIGN * PK_ROWS, LANES), jnp.uint32),
        compiler_params=_cparams(("arbitrary", "arbitrary")),
        name="moe_routed",
    )(item_e, item_start, item_n, sdst, x_pk, *([w1] * MOE_DMA_PARTS + [w3] * MOE_DMA_PARTS + [w2] * MOE_DMA_PARTS))


def _shared_kernel(x_ref, xb_ref, w1_ref, w3_ref, w2_ref, o_ref):
    xb = xb_ref[...]
    a = jax.nn.silu(jnp.dot(xb, w1_ref[...], preferred_element_type=F32))
    a = (a * jnp.dot(xb, w3_ref[...], preferred_element_type=F32)).astype(BF16)
    o_ref[...] = ALPHA * x_ref[...] + jnp.dot(a, w2_ref[...], preferred_element_type=F32)


def _shared_expert(x, xb, ws1, ws3, ws2):
    fixed = lambda i: (0, 0)
    return pl.pallas_call(
        _shared_kernel,
        grid=(T_ALL // TM_SHARED,),
        in_specs=[pl.BlockSpec((TM_SHARED, D_MODEL), lambda i: (i, 0)),
                  pl.BlockSpec((TM_SHARED, D_MODEL), lambda i: (i, 0)),
                  pl.BlockSpec((D_MODEL, D_SHARED), fixed),
                  pl.BlockSpec((D_MODEL, D_SHARED), fixed),
                  pl.BlockSpec((D_SHARED, D_MODEL), fixed)],
        out_specs=pl.BlockSpec((TM_SHARED, D_MODEL), lambda i: (i, 0)),
        out_shape=jax.ShapeDtypeStruct((T_ALL, D_MODEL), F32),
        compiler_params=_cparams(("parallel",)),
        name="shared_expert",
    )(x, xb, ws1, ws3, ws2)


def _combine_kernel(y_ref, gw_ref, base_ref, g_ref, b_ref, x_ref, xb_ref, slab_in, slab_out):
    tm = base_ref.shape[0]
    for s in range(SLAB_ROWS):
        slab_in[pl.ds(s, tm, stride=SLAB_STRIDE), :] = base_ref[:, s * LANES:(s + 1) * LANES]
    for s in range(SLAB_ROWS, SLAB_STRIDE):
        slab_in[pl.ds(s, tm, stride=SLAB_STRIDE), :] = jnp.zeros((tm, LANES), F32)
    base = slab_in[...].reshape(tm, SLAB_STRIDE, LANES)
    lo_rows, hi_rows = slice(0, PK_ROWS), slice(PK_ROWS, SLAB_ROWS)
    acc_lo = base[:, lo_rows, :]
    acc_hi = base[:, hi_rows, :]
    for k in range(TOP_K):
        y_lo, y_hi = _unpack_pair(y_ref[:, k])
        gate = gw_ref[:, k:k + 1, :]
        acc_lo = acc_lo + y_lo * gate
        acc_hi = acc_hi + y_hi * gate

    def token_sum(a, b):
        return jnp.sum(jnp.sum(a, axis=2, keepdims=True) + jnp.sum(b, axis=2, keepdims=True), axis=1, keepdims=True)

    inv_d = 1.0 / D_MODEL
    mu = token_sum(acc_lo, acc_hi) * inv_d
    c_lo, c_hi = acc_lo - mu, acc_hi - mu
    inv_std = lax.rsqrt(token_sum(c_lo * c_lo, c_hi * c_hi) * inv_d + LN_EPS)
    out_lo = c_lo * inv_std * g_ref[:, lo_rows, :] + b_ref[:, lo_rows, :]
    out_hi = c_hi * inv_std * g_ref[:, hi_rows, :] + b_ref[:, hi_rows, :]
    pad = jnp.zeros((tm, SLAB_STRIDE - SLAB_ROWS, LANES), F32)
    slab_out[...] = jnp.concatenate([out_lo, out_hi, pad], axis=1).reshape(tm * SLAB_STRIDE, LANES)
    for s in range(SLAB_ROWS):
        cols = slab_out[pl.ds(s, tm, stride=SLAB_STRIDE), :]
        x_ref[:, s * LANES:(s + 1) * LANES] = cols
        xb_ref[:, s * LANES:(s + 1) * LANES] = cols.astype(BF16)


def _combine(y, gwb, base, g_slab, b_slab):
    row = lambda i: (i, 0)
    return pl.pallas_call(
        _combine_kernel,
        grid=(T_ALL // TM_COMBINE,),
        in_specs=[pl.BlockSpec((TM_COMBINE, TOP_K, PK_ROWS, LANES), lambda i: (i, 0, 0, 0)),
                  pl.BlockSpec((TM_COMBINE, TOP_K, LANES), lambda i: (i, 0, 0)),
                  pl.BlockSpec((TM_COMBINE, D_MODEL), row),
                  pl.BlockSpec((1, SLAB_ROWS, LANES), lambda i: (0, 0, 0)),
                  pl.BlockSpec((1, SLAB_ROWS, LANES), lambda i: (0, 0, 0))],
        out_specs=[pl.BlockSpec((TM_COMBINE, D_MODEL), row),
                   pl.BlockSpec((TM_COMBINE, D_MODEL), row)],
        out_shape=[jax.ShapeDtypeStruct((T_ALL, D_MODEL), F32),
                   jax.ShapeDtypeStruct((T_ALL, D_MODEL), BF16)],
        scratch_shapes=[pltpu.VMEM((TM_COMBINE * SLAB_STRIDE, LANES), F32),
                        pltpu.VMEM((TM_COMBINE * SLAB_STRIDE, LANES), F32)],
        compiler_params=_cparams(("parallel",)),
        name="moe_combine_ln",
    )(y, gwb, base, g_slab, b_slab)


def _ple_kernel(xb_ref, wg_ref, p_ref, wp_ref, x_ref, o_ref):
    gate = jax.nn.sigmoid(jnp.dot(xb_ref[...], wg_ref[...], preferred_element_type=F32))
    proj = jnp.dot(p_ref[...], wp_ref[...], preferred_element_type=F32)
    o_ref[...] = ALPHA * x_ref[...] + gate * proj


def _ple(xb, w_gate, p, w_proj, x):
    return pl.pallas_call(
        _ple_kernel,
        grid=(T_ALL // TM, D_MODEL // TN),
        in_specs=[pl.BlockSpec((TM, D_MODEL), lambda m, n: (m, 0)),
                  pl.BlockSpec((D_MODEL, TN), lambda m, n: (0, n)),
                  pl.BlockSpec((TM, PLE_DIM), lambda m, n: (m, 0)),
                  pl.BlockSpec((PLE_DIM, TN), lambda m, n: (0, n)),
                  pl.BlockSpec((TM, TN), lambda m, n: (m, n))],
        out_specs=pl.BlockSpec((TM, TN), lambda m, n: (m, n)),
        out_shape=jax.ShapeDtypeStruct((T_ALL, D_MODEL), F32),
        compiler_params=_cparams(("parallel", "arbitrary")),
        name="ple",
    )(xb, w_gate, p, w_proj, x)


def _sorted_pos_kernel(e_ref, rk_ref, start_ref, o_ref):
    tm = e_ref.shape[0]
    lane = lax.broadcasted_iota(jnp.int32, (tm, N_EXPERTS), 1)
    cols = []
    for k in range(TOP_K):
        seg = jnp.sum(jnp.where(lane == e_ref[:, k:k + 1], start_ref[...], 0.0), axis=-1, keepdims=True)
        cols.append(seg.astype(jnp.int32) + rk_ref[:, k:k + 1])
    o_ref[...] = _columns_to_lanes(cols, jnp.int32)


def _sorted_pos(e_pad, rank_pad, seg_start):
    row = lambda i: (i, 0)
    return pl.pallas_call(
        _sorted_pos_kernel,
        grid=(T_ALL // TM,),
        in_specs=[pl.BlockSpec((TM, LANES), row),
                  pl.BlockSpec((TM, LANES), row),
                  pl.BlockSpec((1, N_EXPERTS), lambda i: (0, 0))],
        out_specs=pl.BlockSpec((TM, LANES), row),
        out_shape=jax.ShapeDtypeStruct((T_ALL, LANES), jnp.int32),
        compiler_params=_cparams(("parallel",)),
        name="sorted_pos",
    )(e_pad, rank_pad, seg_start)


def _invert_kernel(pos_ref, o_ref):
    def body(g, c):
        for u in range(MOE_DMA_UNROLL):
            i = g * MOE_DMA_UNROLL + u
            o_ref[pos_ref[i]] = i
        return c

    lax.fori_loop(0, N_ASSIGN // MOE_DMA_UNROLL, body, 0)


def _invert_permutation(pos):
    return pl.pallas_call(
        _invert_kernel,
        grid_spec=pltpu.PrefetchScalarGridSpec(
            num_scalar_prefetch=1, grid=(1,), in_specs=[],
            out_specs=pl.BlockSpec(memory_space=pltpu.SMEM)),
        out_shape=jax.ShapeDtypeStruct((N_ASSIGN,), jnp.int32),
        compiler_params=_cparams(("arbitrary",)),
        name="invert_permutation",
    )(pos)


def _dispatch(e_pad, rank_pad, counts):
    sstart = jnp.cumsum(counts) - counts
    pos = _sorted_pos(e_pad, rank_pad, sstart.astype(F32).reshape(1, N_EXPERTS))
    sdst = _invert_permutation(pos[:, :TOP_K].reshape(N_ASSIGN))
    n_items = (counts + MOE_ROWS - 1) // MOE_ROWS
    item_end = jnp.cumsum(n_items)
    total = item_end[-1]
    w = jnp.arange(MOE_ITEMS, dtype=jnp.int32)
    e_of = jnp.minimum(jnp.sum((item_end[None, :] <= w[:, None]).astype(jnp.int32), axis=1), N_EXPERTS - 1)
    local = w - (item_end - n_items)[e_of]
    valid = w < total
    e_last = e_of[jnp.maximum(total - 1, 0)]
    item_e = jnp.where(valid, e_of, e_last)
    item_start = jnp.where(valid, sstart[e_of] + local * MOE_ROWS, 0)
    item_n = jnp.where(valid, jnp.clip(counts[e_of] - local * MOE_ROWS, 0, MOE_ROWS), 0)
    return item_e.astype(jnp.int32), item_start.astype(jnp.int32), item_n.astype(jnp.int32), sdst


def kernel(x_prompt, x_sample, state_pool, p_prompt, p_sample, w_in, ln_v_g, ln_v_b, w_s, b_s, w_pool, pool_scale, w_up_a, w_up_b, w_o, ln_g, ln_b, w_router, b_router, w1, w3, w2, ws1, ws3, ws2, w_ple_gate, w_ple_proj):
    def tokens(a_prompt, a_sample):
        d = a_prompt.shape[-1]
        return jnp.concatenate([a_prompt.reshape(T_PROMPT, d).astype(BF16),
                                a_sample.transpose(1, 0, 2).reshape(T_SAMPLE, d).astype(BF16)], axis=0)

    xb = tokens(x_prompt, x_sample)
    pb = tokens(p_prompt[0], p_sample[0])
    x_prompt_rows = x_prompt.reshape(T_PROMPT, D_MODEL)
    x_sample_rows = x_sample.transpose(1, 0, 2).reshape(T_SAMPLE, D_MODEL)
    row2d = lambda v: v.reshape(1, -1).astype(F32)

    w_in_b = w_in[0].astype(BF16)
    ug = _proj(xb, w_in_b, 0, 2 * CH_A, "gelu", BF16)
    z = _proj(xb, w_in_b, 2 * CH_A, CH_B, "none", F32)
    gates = _proj(xb, w_in_b, 2 * CH_A + CH_B, 2 * D_MODEL, "sigmoid", BF16)

    bias_slab = jnp.repeat(b_s[0].T, HEAD_DIM_A, axis=1).astype(F32)
    wexp = jnp.repeat(w_s[0][:, :DEC_SEQ, :DEC_SEQ].transpose(1, 2, 0).reshape(DEC_SEQ * DEC_SEQ, HEADS_A),
                      HEAD_DIM_A, axis=1).astype(F32)
    lvg, lvb = row2d(ln_v_g[0]), row2d(ln_v_b[0])
    oa_p = _spatial_prompt(ug, w_s[0], bias_slab, lvg, lvb)
    oa_s, vn_s = _spatial_sample(ug, wexp, bias_slab, lvg, lvb)
    o_a = jnp.concatenate([oa_p, oa_s], axis=0)

    w_pool_b = w_pool[0].astype(BF16)
    pscale = row2d(pool_scale[0])
    buf_t = state_pool[0].transpose(1, 0, 2)
    ob_p = _pool_prompt(z, w_pool_b, pscale)
    ob_s = _pool_sample(z, buf_t, w_pool_b, pscale)
    o_b = jnp.concatenate([ob_p, ob_s], axis=0)

    m = _merge(o_a, o_b, w_up_a[0].astype(BF16), w_up_b[0].astype(BF16), gates)
    r1 = _out_proj(m, w_o[0].astype(BF16), x_prompt_rows, x_sample_rows)
    x1, x1b, x1pk, e_pad, gw_pad, rank_pad, counts = _ln_router(
        r1, row2d(ln_g[0, 0]), row2d(ln_b[0, 0]), w_router[0].astype(BF16), row2d(b_router[0]))

    item_e, item_start, item_n, sdst = _dispatch(e_pad, rank_pad, counts[0].astype(jnp.int32))
    y = _moe_routed(x1pk.reshape(T_ALL * PK_ROWS, LANES), w1[0], w3[0], w2[0], item_e, item_start, item_n, sdst)
    base = _shared_expert(x1, x1b, ws1[0].astype(BF16), ws3[0].astype(BF16), ws2[0].astype(BF16))
    gwb = jnp.broadcast_to(gw_pad[:, :TOP_K].reshape(T_ALL, TOP_K, 1), (T_ALL, TOP_K, LANES))
    slab = lambda v: v.reshape(1, SLAB_ROWS, LANES).astype(F32)
    x2, x2b = _combine(y.reshape(T_ALL, TOP_K, PK_ROWS, LANES), gwb, base, slab(ln_g[0, 1]), slab(ln_b[0, 1]))

    r3 = _ple(x2b, w_ple_gate[0].astype(BF16), pb, w_ple_proj[0].astype(BF16), x2)
    y_p, y_s = _ln_split(r3, row2d(ln_g[0, 2]), row2d(ln_b[0, 2]))

    y_prompt = y_p.reshape(BATCH, SEQ, D_MODEL)
    y_sample = y_s.reshape(DEC_SEQ, DEC_BATCH, D_MODEL).transpose(1, 0, 2)
    z_s = z[T_PROMPT:].reshape(DEC_SEQ, DEC_BATCH, CH_B).transpose(1, 0, 2)
    new_pool_prompt = jnp.stack([z[(b + 1) * SEQ - POOL_BUF:(b + 1) * SEQ] for b in range(BATCH)])[None]
    new_pool_sample = jnp.concatenate([state_pool[0][:, DEC_SEQ:], z_s], axis=1)[None]
    new_chunk_v_sample = vn_s.reshape(DEC_SEQ, DEC_BATCH, CH_A).transpose(1, 0, 2)[None]
    return (y_prompt, y_sample, new_pool_prompt, new_pool_sample, new_chunk_v_sample)
```

```python
import functools

import jax
import jax.numpy as jnp
from jax import lax
from jax.experimental import pallas as pl
from jax.experimental.pallas import tpu as pltpu

F32 = jnp.float32
BF16 = jnp.bfloat16

D_MODEL = 4096
BATCH = 4
SEQ = 2048
DEC_BATCH = 128
DEC_SEQ = 4
PAST_LEN = 16384
CHUNK = 128
HEAD_DIM_A = 128
HEADS_A = 16
CH_A = 2048
POOL_WINDOWS = (2, 4, 8, 16)
POOL_GROUPS = 4
CH_B = 2048
POOL_GC = 512
POOL_BUF = 15
N_EXPERTS = 256
TOP_K = 8
N_GROUPS = 8
TOPK_GROUPS = 4
D_EXPERT = 512
D_SHARED = 512
ROUTE_SCALE = 2.5
PLE_DIM = 256
ALPHA = 2.0 ** 0.25
LN_EPS = 1e-5

T_PROMPT = BATCH * SEQ
T_SAMPLE = DEC_BATCH * DEC_SEQ
T_ALL = T_PROMPT + T_SAMPLE
N_ASSIGN = T_ALL * TOP_K

LANES = 128
SLAB_ROWS = D_MODEL // LANES
SLAB_STRIDE = 40
VMEM_LIMIT = 56 * 1024 * 1024

TM = 1088
TN = 512
TM_OUT = 512
TM_LN = 256
TM_SHARED = 256
TM_COMBINE = 64
POOL_TM = 512
POOL_HALO = 16
MOE_ROWS = 384
MOE_BUCKET = 64
MOE_SPLIT = 2
MOE_DQ = D_EXPERT // MOE_SPLIT
MOE_KQ = D_MODEL // MOE_SPLIT
MOE_KSUB = 512
MOE_DMA_PARTS = 1
MOE_ITEMS = N_EXPERTS + N_ASSIGN // MOE_ROWS
HALF = D_MODEL // 2
PK_ROWS = HALF // LANES
PK_STRIDE = 24
MOE_NCHUNK = 512
MOE_DMA_UNROLL = 8
MOE_WAIT_GROUP = 16
TOP_K_SHIFT = 3
HI_MASK = 0xFFFF0000


def _pack_pair(hi, lo):
    h = lax.bitcast_convert_type(hi.astype(BF16).astype(F32), jnp.uint32)
    l = lax.bitcast_convert_type(lo.astype(BF16).astype(F32), jnp.uint32)
    return h | (l >> jnp.uint32(16))


def _unpack_pair(w):
    hi = lax.bitcast_convert_type(w & jnp.uint32(HI_MASK), F32)
    lo = lax.bitcast_convert_type(w << jnp.uint32(16), F32)
    return hi, lo


def _cparams(sem):
    return pltpu.CompilerParams(dimension_semantics=sem, vmem_limit_bytes=VMEM_LIMIT)


def _layer_norm(xf, g, b):
    mu = jnp.mean(xf, axis=-1, keepdims=True)
    xc = xf - mu
    var = jnp.mean(xc * xc, axis=-1, keepdims=True)
    return xc * lax.rsqrt(var + LN_EPS) * g + b


def _proj_kernel(x_ref, w_ref, o_ref, *, act):
    h = jnp.dot(x_ref[...], w_ref[...], preferred_element_type=F32)
    if act == "gelu":
        h = jax.nn.gelu(h, approximate=True)
    elif act == "sigmoid":
        h = jax.nn.sigmoid(h)
    o_ref[...] = h.astype(o_ref.dtype)


def _proj(x, w, col0, ncols, act, out_dtype):
    t, k = x.shape
    off = col0 // TN
    return pl.pallas_call(
        functools.partial(_proj_kernel, act=act),
        grid=(t // TM, ncols // TN),
        in_specs=[pl.BlockSpec((TM, k), lambda m, n: (m, 0)),
                  pl.BlockSpec((k, TN), lambda m, n: (0, n + off))],
        out_specs=pl.BlockSpec((TM, TN), lambda m, n: (m, n)),
        out_shape=jax.ShapeDtypeStruct((t, ncols), out_dtype),
        compiler_params=_cparams(("parallel", "arbitrary")),
        name="in_proj_" + act,
    )(x, w)


def _spatial_prompt_kernel(u_ref, gv_ref, ws_ref, bias_ref, g_ref, b_ref, o_ref):
    vn = _layer_norm(gv_ref[...].astype(F32), g_ref[...], b_ref[...])
    row = lax.broadcasted_iota(jnp.int32, (CHUNK, CHUNK), 0)
    col = lax.broadcasted_iota(jnp.int32, (CHUNK, CHUNK), 1)
    causal = col <= row
    for h in range(HEADS_A):
        sl = slice(h * HEAD_DIM_A, (h + 1) * HEAD_DIM_A)
        w = jnp.where(causal, ws_ref[h], 0.0).astype(BF16)
        s = jnp.dot(w, vn[:, sl].astype(BF16), preferred_element_type=F32) + bias_ref[:, sl]
        o_ref[:, sl] = (u_ref[:, sl].astype(F32) * s).astype(o_ref.dtype)


def _spatial_prompt(ug, w_s, bias_slab, ln_g, ln_b):
    n_chunks = T_PROMPT // CHUNK
    return pl.pallas_call(
        _spatial_prompt_kernel,
        grid=(n_chunks,),
        in_specs=[pl.BlockSpec((CHUNK, CH_A), lambda c: (c, 0)),
                  pl.BlockSpec((CHUNK, CH_A), lambda c: (c, 1)),
                  pl.BlockSpec((HEADS_A, CHUNK, CHUNK), lambda c: (0, 0, 0)),
                  pl.BlockSpec((CHUNK, CH_A), lambda c: (0, 0)),
                  pl.BlockSpec((1, CH_A), lambda c: (0, 0)),
                  pl.BlockSpec((1, CH_A), lambda c: (0, 0))],
        out_specs=pl.BlockSpec((CHUNK, CH_A), lambda c: (c, 0)),
        out_shape=jax.ShapeDtypeStruct((T_PROMPT, CH_A), BF16),
        compiler_params=_cparams(("parallel",)),
        name="spatial_prompt",
    )(ug, ug, w_s, bias_slab, ln_g, ln_b)


def _spatial_sample_kernel(u_ref, gv_ref, wexp_ref, bias_ref, g_ref, b_ref, o_ref, vn_ref):
    vn_ref[...] = _layer_norm(gv_ref[...].astype(F32), g_ref[...], b_ref[...])
    for t in range(DEC_SEQ):
        rows = slice(t * DEC_BATCH, (t + 1) * DEC_BATCH)
        s = jnp.broadcast_to(bias_ref[t:t + 1, :], (DEC_BATCH, CH_A))
        for j in range(t + 1):
            r = t * DEC_SEQ + j
            s = s + wexp_ref[r:r + 1, :] * vn_ref[j * DEC_BATCH:(j + 1) * DEC_BATCH, :]
        o_ref[rows, :] = (u_ref[rows, :].astype(F32) * s).astype(o_ref.dtype)


def _spatial_sample(ug, wexp, bias_slab, ln_g, ln_b):
    blk = T_PROMPT // T_SAMPLE
    return pl.pallas_call(
        _spatial_sample_kernel,
        grid=(1,),
        in_specs=[pl.BlockSpec((T_SAMPLE, CH_A), lambda i: (blk, 0)),
                  pl.BlockSpec((T_SAMPLE, CH_A), lambda i: (blk, 1)),
                  pl.BlockSpec((DEC_SEQ * DEC_SEQ, CH_A), lambda i: (0, 0)),
                  pl.BlockSpec((CHUNK, CH_A), lambda i: (0, 0)),
                  pl.BlockSpec((1, CH_A), lambda i: (0, 0)),
                  pl.BlockSpec((1, CH_A), lambda i: (0, 0))],
        out_specs=[pl.BlockSpec((T_SAMPLE, CH_A), lambda i: (0, 0)),
                   pl.BlockSpec((T_SAMPLE, CH_A), lambda i: (0, 0))],
        out_shape=[jax.ShapeDtypeStruct((T_SAMPLE, CH_A), BF16),
                   jax.ShapeDtypeStruct((T_SAMPLE, CH_A), F32)],
        compiler_params=_cparams(("arbitrary",)),
        name="spatial_sample",
    )(ug, ug, wexp, bias_slab, ln_g, ln_b)


def _pool_project(d_of_group, wp_ref, scale_ref, o_ref):
    for g in range(POOL_GROUPS):
        sl = slice(g * POOL_GC, (g + 1) * POOL_GC)
        y = jnp.dot(d_of_group(g).astype(BF16), wp_ref[g], preferred_element_type=F32)
        o_ref[:, sl] = (y * scale_ref[:, sl]).astype(o_ref.dtype)


def _pool_prompt_kernel(z_ref, halo_ref, wp_ref, scale_ref, o_ref, cat_ref):
    i = pl.program_id(1)
    halo = halo_ref[...]
    cat_ref[0:POOL_HALO, :] = jnp.where(i == 0, jnp.zeros_like(halo), halo)
    cat_ref[POOL_HALO:, :] = z_ref[...]
    pos = i * POOL_TM + lax.broadcasted_iota(jnp.int32, (POOL_TM, POOL_GC), 0)

    def d_of_group(g):
        win = POOL_WINDOWS[g]
        sl = slice(g * POOL_GC, (g + 1) * POOL_GC)
        acc = cat_ref[POOL_HALO:POOL_HALO + POOL_TM, sl]
        for k in range(1, win):
            acc = acc + cat_ref[POOL_HALO - k:POOL_HALO - k + POOL_TM, sl]
        cnt = jnp.minimum(pos + 1, win).astype(F32)
        return acc / cnt - z_ref[:, sl]

    _pool_project(d_of_group, wp_ref, scale_ref, o_ref)


def _pool_prompt(z, w_pool, pool_scale):
    tiles = SEQ // POOL_TM
    per_tile = POOL_TM // POOL_HALO
    return pl.pallas_call(
        _pool_prompt_kernel,
        grid=(BATCH, tiles),
        in_specs=[pl.BlockSpec((POOL_TM, CH_B), lambda b, i: (b * tiles + i, 0)),
                  pl.BlockSpec((POOL_HALO, CH_B),
                               lambda b, i: (jnp.maximum((b * tiles + i) * per_tile - 1, 0), 0)),
                  pl.BlockSpec((POOL_GROUPS, POOL_GC, POOL_GC), lambda b, i: (0, 0, 0)),
                  pl.BlockSpec((1, CH_B), lambda b, i: (0, 0))],
        out_specs=pl.BlockSpec((POOL_TM, CH_B), lambda b, i: (b * tiles + i, 0)),
        out_shape=jax.ShapeDtypeStruct((T_PROMPT, CH_B), BF16),
        scratch_shapes=[pltpu.VMEM((POOL_HALO + POOL_TM, CH_B), F32)],
        compiler_params=_cparams(("parallel", "arbitrary")),
        name="pool_prompt",
    )(z, z, w_pool, pool_scale)


def _pool_sample_kernel(z_ref, buf_ref, wp_ref, scale_ref, o_ref, d_ref):
    def cat_row(r, sl):
        if r < POOL_BUF:
            return buf_ref[r, :, sl]
        t = r - POOL_BUF
        return z_ref[t * DEC_BATCH:(t + 1) * DEC_BATCH, sl]

    for g in range(POOL_GROUPS):
        win = POOL_WINDOWS[g]
        sl = slice(g * POOL_GC, (g + 1) * POOL_GC)
        for t in range(DEC_SEQ):
            acc = cat_row(POOL_BUF + t, sl)
            for k in range(1, win):
                acc = acc + cat_row(POOL_BUF + t - k, sl)
            cnt = float(min(PAST_LEN + t + 1, win))
            d_ref[t * DEC_BATCH:(t + 1) * DEC_BATCH, sl] = acc / cnt - cat_row(POOL_BUF + t, sl)

    _pool_project(lambda g: d_ref[:, g * POOL_GC:(g + 1) * POOL_GC], wp_ref, scale_ref, o_ref)


def _pool_sample(z, buf_t, w_pool, pool_scale):
    blk = T_PROMPT // T_SAMPLE
    return pl.pallas_call(
        _pool_sample_kernel,
        grid=(1,),
        in_specs=[pl.BlockSpec((T_SAMPLE, CH_B), lambda i: (blk, 0)),
                  pl.BlockSpec((POOL_BUF, DEC_BATCH, CH_B), lambda i: (0, 0, 0)),
                  pl.BlockSpec((POOL_GROUPS, POOL_GC, POOL_GC), lambda i: (0, 0, 0)),
                  pl.BlockSpec((1, CH_B), lambda i: (0, 0))],
        out_specs=pl.BlockSpec((T_SAMPLE, CH_B), lambda i: (0, 0)),
        out_shape=jax.ShapeDtypeStruct((T_SAMPLE, CH_B), BF16),
        scratch_shapes=[pltpu.VMEM((T_SAMPLE, CH_B), F32)],
        compiler_params=_cparams(("arbitrary",)),
        name="pool_sample",
    )(z, buf_t, w_pool, pool_scale)


def _merge_kernel(oa_ref, ob_ref, wa_ref, wb_ref, ga_ref, gb_ref, o_ref):
    a = jnp.dot(oa_ref[...], wa_ref[...], preferred_element_type=F32)
    b = jnp.dot(ob_ref[...], wb_ref[...], preferred_element_type=F32)
    o_ref[...] = (ga_ref[...].astype(F32) * a + gb_ref[...].astype(F32) * b).astype(o_ref.dtype)


def _merge(o_a, o_b, w_up_a, w_up_b, gates):
    nb = D_MODEL // TN
    return pl.pallas_call(
        _merge_kernel,
        grid=(T_ALL // TM, nb),
        in_specs=[pl.BlockSpec((TM, CH_A), lambda m, n: (m, 0)),
                  pl.BlockSpec((TM, CH_B), lambda m, n: (m, 0)),
                  pl.BlockSpec((CH_A, TN), lambda m, n: (0, n)),
                  pl.BlockSpec((CH_B, TN), lambda m, n: (0, n)),
                  pl.BlockSpec((TM, TN), lambda m, n: (m, n)),
                  pl.BlockSpec((TM, TN), lambda m, n: (m, n + nb))],
        out_specs=pl.BlockSpec((TM, TN), lambda m, n: (m, n)),
        out_shape=jax.ShapeDtypeStruct((T_ALL, D_MODEL), BF16),
        compiler_params=_cparams(("parallel", "arbitrary")),
        name="merge",
    )(o_a, o_b, w_up_a, w_up_b, gates, gates)


def _out_proj_kernel(m_ref, w_ref, xp_ref, xs_ref, o_ref):
    x = jnp.where(pl.program_id(0) < T_PROMPT // TM_OUT, xp_ref[...], xs_ref[...])
    o_ref[...] = ALPHA * x + jnp.dot(m_ref[...], w_ref[...], preferred_element_type=F32)


def _out_proj(m, w_o, x_prompt, x_sample):
    n_prompt = T_PROMPT // TM_OUT
    return pl.pallas_call(
        _out_proj_kernel,
        grid=(T_ALL // TM_OUT, D_MODEL // TN),
        in_specs=[pl.BlockSpec((TM_OUT, D_MODEL), lambda m_, n: (m_, 0)),
                  pl.BlockSpec((D_MODEL, TN), lambda m_, n: (0, n)),
                  pl.BlockSpec((TM_OUT, TN), lambda m_, n: (jnp.minimum(m_, n_prompt - 1), n)),
                  pl.BlockSpec((TM_OUT, TN), lambda m_, n: (0, n))],
        out_specs=pl.BlockSpec((TM_OUT, TN), lambda m_, n: (m_, n)),
        out_shape=jax.ShapeDtypeStruct((T_ALL, D_MODEL), F32),
        compiler_params=_cparams(("parallel", "arbitrary")),
        name="out_proj",
    )(m, w_o, x_prompt, x_sample)


def _route_tile(scores, bias, carry):
    tm = scores.shape[0]
    lane = lax.broadcasted_iota(jnp.int32, (tm, N_EXPERTS), 1)
    per_group = N_EXPERTS // N_GROUPS
    neg = -jnp.inf
    biased = scores + bias

    def first_argmax(v):
        m = jnp.max(v, axis=-1, keepdims=True)
        return m, jnp.min(jnp.where(v == m, lane, N_EXPERTS), axis=-1, keepdims=True)

    in_group = [jnp.logical_and(lane >= g * per_group, lane < (g + 1) * per_group) for g in range(N_GROUPS)]
    gscore = []
    for g in range(N_GROUPS):
        v = jnp.where(in_group[g], biased, neg)
        m1, i1 = first_argmax(v)
        m2 = jnp.max(jnp.where(lane == i1, neg, v), axis=-1, keepdims=True)
        gscore.append(m1 + m2)
    group_kept = jnp.zeros((tm, N_EXPERTS), jnp.int32)
    for g in range(N_GROUPS):
        beaten_by = jnp.zeros((tm, 1), jnp.int32)
        for h in range(N_GROUPS):
            if h == g:
                continue
            wins = gscore[h] >= gscore[g] if h < g else gscore[h] > gscore[g]
            beaten_by = beaten_by + jnp.where(wins, 1, 0)
        kept = jnp.where(beaten_by < TOPK_GROUPS, 1, 0)
        group_kept = group_kept + jnp.where(in_group[g], kept, 0)
    masked = jnp.where(group_kept > 0, biased, neg)

    idx, gate = [], []
    sel_f = jnp.zeros((tm, N_EXPERTS), F32)
    for _ in range(TOP_K):
        _, i = first_argmax(masked)
        hit = lane == i
        idx.append(i)
        gate.append(jnp.sum(jnp.where(hit, scores, 0.0), axis=-1, keepdims=True))
        sel_f = sel_f + jnp.where(hit, 1.0, 0.0)
        masked = jnp.where(hit, neg, masked)
    total = gate[0]
    for k in range(1, TOP_K):
        total = total + gate[k]
    gate = [gk / total * ROUTE_SCALE for gk in gate]

    r_i = lax.broadcasted_iota(jnp.int32, (tm, tm), 0)
    c_i = lax.broadcasted_iota(jnp.int32, (tm, tm), 1)
    lower = jnp.where(c_i < r_i, 1.0, 0.0).astype(BF16)
    rank_full = carry + jnp.dot(lower, sel_f.astype(BF16), preferred_element_type=F32)
    rank = [jnp.sum(jnp.where(lane == i, rank_full, 0.0), axis=-1, keepdims=True) for i in idx]
    return idx, gate, rank, jnp.sum(sel_f, axis=0, keepdims=True)


def _columns_to_lanes(cols, dtype):
    tm = cols[0].shape[0]
    lane = lax.broadcasted_iota(jnp.int32, (tm, LANES), 1)
    out = jnp.zeros((tm, LANES), dtype)
    for k, c in enumerate(cols):
        out = jnp.where(lane == k, c.astype(dtype), out)
    return out


def _ln_router_kernel(r_ref, g_ref, b_ref, wr_ref, br_ref, x_ref, xb_ref, xpk_ref, e_ref, gw_ref, rk_ref, cnt_ref):
    @pl.when(pl.program_id(0) == 0)
    def _init():
        cnt_ref[...] = jnp.zeros_like(cnt_ref)

    x1 = _layer_norm(r_ref[...], g_ref[...], b_ref[...])
    x_ref[...] = x1
    xb = x1.astype(BF16)
    xb_ref[...] = xb
    xpk_ref[...] = _pack_pair(x1[:, :HALF], x1[:, HALF:])
    scores = jax.nn.sigmoid(jnp.dot(xb, wr_ref[...], preferred_element_type=F32))
    idx, gate, rank, counts = _route_tile(scores, br_ref[...], cnt_ref[...])
    e_ref[...] = _columns_to_lanes(idx, jnp.int32)
    gw_ref[...] = _columns_to_lanes(gate, F32)
    rk_ref[...] = _columns_to_lanes(rank, jnp.int32)
    cnt_ref[...] = cnt_ref[...] + counts


def _ln_router(r, g, b, w_router, b_router):
    row = lambda i: (i, 0)
    fixed = lambda i: (0, 0)
    return pl.pallas_call(
        _ln_router_kernel,
        grid=(T_ALL // TM_LN,),
        in_specs=[pl.BlockSpec((TM_LN, D_MODEL), row),
                  pl.BlockSpec((1, D_MODEL), fixed),
                  pl.BlockSpec((1, D_MODEL), fixed),
                  pl.BlockSpec((D_MODEL, N_EXPERTS), fixed),
                  pl.BlockSpec((1, N_EXPERTS), fixed)],
        out_specs=[pl.BlockSpec((TM_LN, D_MODEL), row),
                   pl.BlockSpec((TM_LN, D_MODEL), row),
                   pl.BlockSpec((TM_LN, HALF), row),
                   pl.BlockSpec((TM_LN, LANES), row),
                   pl.BlockSpec((TM_LN, LANES), row),
                   pl.BlockSpec((TM_LN, LANES), row),
                   pl.BlockSpec((1, N_EXPERTS), fixed)],
        out_shape=[jax.ShapeDtypeStruct((T_ALL, D_MODEL), F32),
                   jax.ShapeDtypeStruct((T_ALL, D_MODEL), BF16),
                   jax.ShapeDtypeStruct((T_ALL, HALF), jnp.uint32),
                   jax.ShapeDtypeStruct((T_ALL, LANES), jnp.int32),
                   jax.ShapeDtypeStruct((T_ALL, LANES), F32),
                   jax.ShapeDtypeStruct((T_ALL, LANES), jnp.int32),
                   jax.ShapeDtypeStruct((1, N_EXPERTS), F32)],
        compiler_params=_cparams(("arbitrary",)),
        name="ln_router",
    )(r, g, b, w_router, b_router)


def _ln_split_kernel(r_ref, g_ref, b_ref, op_ref, os_ref):
    i = pl.program_id(0)
    y = _layer_norm(r_ref[...], g_ref[...], b_ref[...])

    @pl.when(i < T_PROMPT // TM_LN)
    def _prompt():
        op_ref[...] = y

    @pl.when(i >= T_PROMPT // TM_LN)
    def _sample():
        os_ref[...] = y


def _ln_split(r, g, b):
    fixed = lambda i: (0, 0)
    n_prompt = T_PROMPT // TM_LN
    return pl.pallas_call(
        _ln_split_kernel,
        grid=(T_ALL // TM_LN,),
        in_specs=[pl.BlockSpec((TM_LN, D_MODEL), lambda i: (i, 0)),
                  pl.BlockSpec((1, D_MODEL), fixed),
                  pl.BlockSpec((1, D_MODEL), fixed)],
        out_specs=[pl.BlockSpec((TM_LN, D_MODEL), lambda i: (jnp.minimum(i, n_prompt - 1), 0)),
                   pl.BlockSpec((TM_LN, D_MODEL), lambda i: (jnp.maximum(i - n_prompt, 0), 0))],
        out_shape=[jax.ShapeDtypeStruct((T_PROMPT, D_MODEL), F32),
                   jax.ShapeDtypeStruct((T_SAMPLE, D_MODEL), F32)],
        compiler_params=_cparams(("arbitrary",)),
        name="ln_final",
    )(r, g, b)


def _moe_kernel(item_e, item_start, item_n, sdst, x_hbm, *refs):
    w1_refs = refs[:MOE_DMA_PARTS]
    w3_refs = refs[MOE_DMA_PARTS:2 * MOE_DMA_PARTS]
    w2_refs = refs[2 * MOE_DMA_PARTS:3 * MOE_DMA_PARTS]
    y_hbm, xslab, xb, w13b, w2b, hacc, yslab, gsem, ssem = refs[3 * MOE_DMA_PARTS:]
    _moe_body(item_start, item_n, sdst, x_hbm, w1_refs, w3_refs, w2_refs, y_hbm,
              xslab, xb, w13b, w2b, hacc, yslab, gsem, ssem)


def _moe_body(item_start, item_n, sdst, x_hbm, w1_refs, w3_refs, w2_refs, y_hbm,
              xslab, xb, w13b, w2b, hacc, yslab, gsem, ssem):
    w = pl.program_id(0)
    q = pl.program_id(1)
    n = item_n[w]
    start = item_start[w]
    slot = w % 2
    last_q = MOE_SPLIT - 1

    def x_copy(tok, r, slot_):
        return pltpu.make_async_copy(x_hbm.at[pl.ds(tok * PK_ROWS, PK_ROWS), :],
                                     xslab.at[slot_, pl.ds(r * PK_STRIDE, PK_ROWS), :],
                                     gsem.at[slot_])

    def y_copy(r, dst):
        return pltpu.make_async_copy(yslab.at[pl.ds(r * PK_STRIDE, PK_ROWS), :],
                                     y_hbm.at[pl.ds(dst * PK_ROWS, PK_ROWS), :], ssem)

    def for_each_row(count, row_fn):
        groups = count // MOE_DMA_UNROLL

        def group_body(g, c):
            for u in range(MOE_DMA_UNROLL):
                row_fn(g * MOE_DMA_UNROLL + u)
            return c

        def tail_body(r, c):
            row_fn(r)
            return c

        lax.fori_loop(0, groups, group_body, 0)
        lax.fori_loop(groups * MOE_DMA_UNROLL, count, tail_body, 0)

    def start_gather(item, slot_):
        base = item_start[item]
        for_each_row(item_n[item],
                     lambda r: x_copy(lax.shift_right_logical(sdst[base + r], TOP_K_SHIFT), r, slot_).start())

    def wait_rows(count, group_wait, row_wait):
        groups = count // MOE_WAIT_GROUP

        def group_body(g, c):
            group_wait()
            return c

        def tail_body(r, c):
            row_wait(r)
            return c

        lax.fori_loop(0, groups, group_body, 0)
        lax.fori_loop(groups * MOE_WAIT_GROUP, count, tail_body, 0)

    group_rows = MOE_WAIT_GROUP * PK_ROWS

    def wait_gather(count, slot_):
        wait_rows(count,
                  lambda: pltpu.make_async_copy(x_hbm.at[pl.ds(0, group_rows), :],
                                                xslab.at[slot_, pl.ds(0, group_rows), :], gsem.at[slot_]).wait(),
                  lambda r: x_copy(0, r, slot_).wait())

    def wait_scatter(count):
        wait_rows(count,
                  lambda: pltpu.make_async_copy(yslab.at[pl.ds(0, group_rows), :],
                                                y_hbm.at[pl.ds(0, group_rows), :], ssem).wait(),
                  lambda r: y_copy(r, 0).wait())

    @pl.when(jnp.logical_and(w == 0, q == 0))
    def _first():
        xslab[...] = jnp.zeros_like(xslab)
        yslab[...] = jnp.zeros_like(yslab)
        hacc[...] = jnp.zeros_like(hacc)
        start_gather(0, 0)

    @pl.when(jnp.logical_and(q == 0, n > 0))
    def _wait_rows():
        wait_gather(n, slot)

    @pl.when(jnp.logical_and(q == 1, w + 1 < MOE_ITEMS))
    def _prefetch_rows():
        start_gather(w + 1, 1 - slot)

    @pl.when(jnp.logical_and(q == last_q, w > 0))
    def _wait_prev_scatter():
        wait_scatter(item_n[w - 1])

    @pl.when(n > 0)
    def _compute():
        nb = (n + MOE_BUCKET - 1) // MOE_BUCKET
        for b in range(1, MOE_ROWS // MOE_BUCKET + 1):
            rows = b * MOE_BUCKET

            @pl.when(nb == b)
            def _bucket(rows=rows):
                @pl.when(q == 0)
                def _unpack():
                    for s in range(PK_ROWS):
                        hi, lo = _unpack_pair(xslab[slot, pl.ds(s, rows, stride=PK_STRIDE), :])
                        for col, val in ((s * LANES, hi), (HALF + s * LANES, lo)):
                            xb[col // MOE_KQ, 0:rows, col % MOE_KQ:col % MOE_KQ + LANES] = val.astype(BF16)

                part = None
                kp, dp = MOE_KQ // MOE_DMA_PARTS, MOE_DQ // MOE_DMA_PARTS
                piece = min(kp, MOE_KSUB)
                for c in range(MOE_KQ // MOE_KSUB):
                    ks = slice(c * MOE_KSUB, (c + 1) * MOE_KSUB)
                    for r0 in range(c * MOE_KSUB, (c + 1) * MOE_KSUB, piece):
                        j, off = r0 // kp, r0 % kp
                        w13b[r0:r0 + piece, :D_EXPERT] = w1_refs[j][off:off + piece, :].astype(BF16)
                        w13b[r0:r0 + piece, D_EXPERT:] = w3_refs[j][off:off + piece, :].astype(BF16)
                    d = jnp.dot(xb[q, 0:rows, ks], w13b[ks, :], preferred_element_type=F32)
                    part = d if part is None else part + d
                h = jnp.where(q == 0, part, hacc[0:rows, :] + part)
                hacc[0:rows, :] = h
                for j in range(MOE_DMA_PARTS):
                    w2b[pl.ds(pl.multiple_of(q * MOE_DQ + j * dp, dp), dp), :] = w2_refs[j][...].astype(BF16)

                @pl.when(q == last_q)
                def _down():
                    a = (jax.nn.silu(h[:, :D_EXPERT]) * h[:, D_EXPERT:]).astype(BF16)
                    per_chunk = MOE_NCHUNK // LANES
                    for c in range(HALF // MOE_NCHUNK):
                        lo_cols = slice(c * MOE_NCHUNK, (c + 1) * MOE_NCHUNK)
                        hi_cols = slice(HALF + c * MOE_NCHUNK, HALF + (c + 1) * MOE_NCHUNK)
                        pk = _pack_pair(jnp.dot(a, w2b[:, lo_cols], preferred_element_type=F32),
                                        jnp.dot(a, w2b[:, hi_cols], preferred_element_type=F32))
                        for s in range(per_chunk):
                            yslab[pl.ds(c * per_chunk + s, rows, stride=PK_STRIDE), :] = pk[:, s * LANES:(s + 1) * LANES]

    @pl.when(jnp.logical_and(q == last_q, n > 0))
    def _scatter():
        for_each_row(n, lambda r: y_copy(r, sdst[start + r]).start())

    @pl.when(jnp.logical_and(jnp.logical_and(q == last_q, w == MOE_ITEMS - 1), n > 0))
    def _drain():
        wait_scatter(n)


def _moe_routed(x_pk, w1, w3, w2, item_e, item_start, item_n, sdst):
    def weight_block(part):
        def index_map(w, q, item_e, item_start, item_n, sdst):
            q_eff = jnp.where(item_n[w] > 0, q, MOE_SPLIT - 1)
            return (item_e[w], q_eff * MOE_DMA_PARTS + part, 0)

        return index_map

    grid_spec = pltpu.PrefetchScalarGridSpec(
        num_scalar_prefetch=4,
        grid=(MOE_ITEMS, MOE_SPLIT),
        in_specs=[pl.BlockSpec(memory_space=pl.ANY)]
        + [pl.BlockSpec((None, MOE_KQ // MOE_DMA_PARTS, D_EXPERT), weight_block(j)) for j in range(MOE_DMA_PARTS)]
        + [pl.BlockSpec((None, MOE_KQ // MOE_DMA_PARTS, D_EXPERT), weight_block(j)) for j in range(MOE_DMA_PARTS)]
        + [pl.BlockSpec((None, MOE_DQ // MOE_DMA_PARTS, D_MODEL), weight_block(j)) for j in range(MOE_DMA_PARTS)],
        out_specs=pl.BlockSpec(memory_space=pl.ANY),
        scratch_shapes=[
            pltpu.VMEM((2, MOE_ROWS * PK_STRIDE, LANES), jnp.uint32),
            pltpu.VMEM((MOE_SPLIT, MOE_ROWS, MOE_KQ), BF16),
            pltpu.VMEM((MOE_KQ, 2 * D_EXPERT), BF16),
            pltpu.VMEM((D_EXPERT, D_MODEL), BF16),
            pltpu.VMEM((MOE_ROWS, 2 * D_EXPERT), F32),
            pltpu.VMEM((MOE_ROWS * PK_STRIDE, LANES), jnp.uint32),
            pltpu.SemaphoreType.DMA((2,)),
            pltpu.SemaphoreType.DMA(()),
        ],
    )
    return pl.pallas_call(
        _moe_kernel,
        grid_spec=grid_spec,
        out_shape=jax.ShapeDtypeStruct((N_ASSIGN * PK_ROWS, LANES), jnp.uint32),
        compiler_params=_cparams(("arbitrary", "arbitrary")),
        name="moe_routed",
    )(item_e, item_start, item_n, sdst, x_pk, *([w1] * MOE_DMA_PARTS + [w3] * MOE_DMA_PARTS + [w2] * MOE_DMA_PARTS))


def _shared_kernel(x_ref, xb_ref, w1_ref, w3_ref, w2_ref, o_ref):
    xb = xb_ref[...]
    a = jax.nn.silu(jnp.dot(xb, w1_ref[...], preferred_element_type=F32))
    a = (a * jnp.dot(xb, w3_ref[...], preferred_element_type=F32)).astype(BF16)
    o_ref[...] = ALPHA * x_ref[...] + jnp.dot(a, w2_ref[...], preferred_element_type=F32)


def _shared_expert(x, xb, ws1, ws3, ws2):
    fixed = lambda i: (0, 0)
    return pl.pallas_call(
        _shared_kernel,
        grid=(T_ALL // TM_SHARED,),
        in_specs=[pl.BlockSpec((TM_SHARED, D_MODEL), lambda i: (i, 0)),
                  pl.BlockSpec((TM_SHARED, D_MODEL), lambda i: (i, 0)),
                  pl.BlockSpec((D_MODEL, D_SHARED), fixed),
                  pl.BlockSpec((D_MODEL, D_SHARED), fixed),
                  pl.BlockSpec((D_SHARED, D_MODEL), fixed)],
        out_specs=pl.BlockSpec((TM_SHARED, D_MODEL), lambda i: (i, 0)),
        out_shape=jax.ShapeDtypeStruct((T_ALL, D_MODEL), F32),
        compiler_params=_cparams(("parallel",)),
        name="shared_expert",
    )(x, xb, ws1, ws3, ws2)


def _combine_kernel(y_ref, gw_ref, base_ref, g_ref, b_ref, x_ref, xb_ref, slab_in, slab_out):
    tm = base_ref.shape[0]
    for s in range(SLAB_ROWS):
        slab_in[pl.ds(s, tm, stride=SLAB_STRIDE), :] = base_ref[:, s * LANES:(s + 1) * LANES]
    for s in range(SLAB_ROWS, SLAB_STRIDE):
        slab_in[pl.ds(s, tm, stride=SLAB_STRIDE), :] = jnp.zeros((tm, LANES), F32)
    base = slab_in[...].reshape(tm, SLAB_STRIDE, LANES)
    lo_rows, hi_rows = slice(0, PK_ROWS), slice(PK_ROWS, SLAB_ROWS)
    acc_lo = base[:, lo_rows, :]
    acc_hi = base[:, hi_rows, :]
    for k in range(TOP_K):
        y_lo, y_hi = _unpack_pair(y_ref[:, k])
        gate = gw_ref[:, k:k + 1, :]
        acc_lo = acc_lo + y_lo * gate
        acc_hi = acc_hi + y_hi * gate

    def token_sum(a, b):
        return jnp.sum(jnp.sum(a, axis=2, keepdims=True) + jnp.sum(b, axis=2, keepdims=True), axis=1, keepdims=True)

    inv_d = 1.0 / D_MODEL
    mu = token_sum(acc_lo, acc_hi) * inv_d
    c_lo, c_hi = acc_lo - mu, acc_hi - mu
    inv_std = lax.rsqrt(token_sum(c_lo * c_lo, c_hi * c_hi) * inv_d + LN_EPS)
    out_lo = c_lo * inv_std * g_ref[:, lo_rows, :] + b_ref[:, lo_rows, :]
    out_hi = c_hi * inv_std * g_ref[:, hi_rows, :] + b_ref[:, hi_rows, :]
    pad = jnp.zeros((tm, SLAB_STRIDE - SLAB_ROWS, LANES), F32)
    slab_out[...] = jnp.concatenate([out_lo, out_hi, pad], axis=1).reshape(tm * SLAB_STRIDE, LANES)
    for s in range(SLAB_ROWS):
        cols = slab_out[pl.ds(s, tm, stride=SLAB_STRIDE), :]
        x_ref[:, s * LANES:(s + 1) * LANES] = cols
        xb_ref[:, s * LANES:(s + 1) * LANES] = cols.astype(BF16)


def _combine(y, gwb, base, g_slab, b_slab):
    row = lambda i: (i, 0)
    return pl.pallas_call(
        _combine_kernel,
        grid=(T_ALL // TM_COMBINE,),
        in_specs=[pl.BlockSpec((TM_COMBINE, TOP_K, PK_ROWS, LANES), lambda i: (i, 0, 0, 0)),
                  pl.BlockSpec((TM_COMBINE, TOP_K, LANES), lambda i: (i, 0, 0)),
                  pl.BlockSpec((TM_COMBINE, D_MODEL), row),
                  pl.BlockSpec((1, SLAB_ROWS, LANES), lambda i: (0, 0, 0)),
                  pl.BlockSpec((1, SLAB_ROWS, LANES), lambda i: (0, 0, 0))],
        out_specs=[pl.BlockSpec((TM_COMBINE, D_MODEL), row),
                   pl.BlockSpec((TM_COMBINE, D_MODEL), row)],
        out_shape=[jax.ShapeDtypeStruct((T_ALL, D_MODEL), F32),
                   jax.ShapeDtypeStruct((T_ALL, D_MODEL), BF16)],
        scratch_shapes=[pltpu.VMEM((TM_COMBINE * SLAB_STRIDE, LANES), F32),
                        pltpu.VMEM((TM_COMBINE * SLAB_STRIDE, LANES), F32)],
        compiler_params=_cparams(("parallel",)),
        name="moe_combine_ln",
    )(y, gwb, base, g_slab, b_slab)


def _ple_kernel(xb_ref, wg_ref, p_ref, wp_ref, x_ref, o_ref):
    gate = jax.nn.sigmoid(jnp.dot(xb_ref[...], wg_ref[...], preferred_element_type=F32))
    proj = jnp.dot(p_ref[...], wp_ref[...], preferred_element_type=F32)
    o_ref[...] = ALPHA * x_ref[...] + gate * proj


def _ple(xb, w_gate, p, w_proj, x):
    return pl.pallas_call(
        _ple_kernel,
        grid=(T_ALL // TM, D_MODEL // TN),
        in_specs=[pl.BlockSpec((TM, D_MODEL), lambda m, n: (m, 0)),
                  pl.BlockSpec((D_MODEL, TN), lambda m, n: (0, n)),
                  pl.BlockSpec((TM, PLE_DIM), lambda m, n: (m, 0)),
                  pl.BlockSpec((PLE_DIM, TN), lambda m, n: (0, n)),
                  pl.BlockSpec((TM, TN), lambda m, n: (m, n))],
        out_specs=pl.BlockSpec((TM, TN), lambda m, n: (m, n)),
        out_shape=jax.ShapeDtypeStruct((T_ALL, D_MODEL), F32),
        compiler_params=_cparams(("parallel", "arbitrary")),
        name="ple",
    )(xb, w_gate, p, w_proj, x)


def _sorted_pos_kernel(e_ref, rk_ref, start_ref, o_ref):
    tm = e_ref.shape[0]
    lane = lax.broadcasted_iota(jnp.int32, (tm, N_EXPERTS), 1)
    cols = []
    for k in range(TOP_K):
        seg = jnp.sum(jnp.where(lane == e_ref[:, k:k + 1], start_ref[...], 0.0), axis=-1, keepdims=True)
        cols.append(seg.astype(jnp.int32) + rk_ref[:, k:k + 1])
    o_ref[...] = _columns_to_lanes(cols, jnp.int32)


def _sorted_pos(e_pad, rank_pad, seg_start):
    row = lambda i: (i, 0)
    return pl.pallas_call(
        _sorted_pos_kernel,
        grid=(T_ALL // TM,),
        in_specs=[pl.BlockSpec((TM, LANES), row),
                  pl.BlockSpec((TM, LANES), row),
                  pl.BlockSpec((1, N_EXPERTS), lambda i: (0, 0))],
        out_specs=pl.BlockSpec((TM, LANES), row),
        out_shape=jax.ShapeDtypeStruct((T_ALL, LANES), jnp.int32),
        compiler_params=_cparams(("parallel",)),
        name="sorted_pos",
    )(e_pad, rank_pad, seg_start)


def _invert_kernel(pos_ref, o_ref):
    def body(g, c):
        for u in range(MOE_DMA_UNROLL):
            i = g * MOE_DMA_UNROLL + u
            o_ref[pos_ref[i]] = i
        return c

    lax.fori_loop(0, N_ASSIGN // MOE_DMA_UNROLL, body, 0)


def _invert_permutation(pos):
    return pl.pallas_call(
        _invert_kernel,
        grid_spec=pltpu.PrefetchScalarGridSpec(
            num_scalar_prefetch=1, grid=(1,), in_specs=[],
            out_specs=pl.BlockSpec(memory_space=pltpu.SMEM)),
        out_shape=jax.ShapeDtypeStruct((N_ASSIGN,), jnp.int32),
        compiler_params=_cparams(("arbitrary",)),
        name="invert_permutation",
    )(pos)


def _dispatch(e_pad, rank_pad, counts):
    sstart = jnp.cumsum(counts) - counts
    pos = _sorted_pos(e_pad, rank_pad, sstart.astype(F32).reshape(1, N_EXPERTS))
    sdst = _invert_permutation(pos[:, :TOP_K].reshape(N_ASSIGN))
    n_items = (counts + MOE_ROWS - 1) // MOE_ROWS
    item_end = jnp.cumsum(n_items)
    total = item_end[-1]
    w = jnp.arange(MOE_ITEMS, dtype=jnp.int32)
    e_of = jnp.minimum(jnp.sum((item_end[None, :] <= w[:, None]).astype(jnp.int32), axis=1), N_EXPERTS - 1)
    local = w - (item_end - n_items)[e_of]
    valid = w < total
    e_last = e_of[jnp.maximum(total - 1, 0)]
    item_e = jnp.where(valid, e_of, e_last)
    item_start = jnp.where(valid, sstart[e_of] + local * MOE_ROWS, 0)
    item_n = jnp.where(valid, jnp.clip(counts[e_of] - local * MOE_ROWS, 0, MOE_ROWS), 0)
    return item_e.astype(jnp.int32), item_start.astype(jnp.int32), item_n.astype(jnp.int32), sdst


def kernel(x_prompt, x_sample, state_pool, p_prompt, p_sample, w_in, ln_v_g, ln_v_b, w_s, b_s, w_pool, pool_scale, w_up_a, w_up_b, w_o, ln_g, ln_b, w_router, b_router, w1, w3, w2, ws1, ws3, ws2, w_ple_gate, w_ple_proj):
    def tokens(a_prompt, a_sample):
        d = a_prompt.shape[-1]
        return jnp.concatenate([a_prompt.reshape(T_PROMPT, d).astype(BF16),
                                a_sample.transpose(1, 0, 2).reshape(T_SAMPLE, d).astype(BF16)], axis=0)

    xb = tokens(x_prompt, x_sample)
    pb = tokens(p_prompt[0], p_sample[0])
    x_prompt_rows = x_prompt.reshape(T_PROMPT, D_MODEL)
    x_sample_rows = x_sample.transpose(1, 0, 2).reshape(T_SAMPLE, D_MODEL)
    row2d = lambda v: v.reshape(1, -1).astype(F32)

    w_in_b = w_in[0].astype(BF16)
    ug = _proj(xb, w_in_b, 0, 2 * CH_A, "gelu", BF16)
    z = _proj(xb, w_in_b, 2 * CH_A, CH_B, "none", F32)
    gates = _proj(xb, w_in_b, 2 * CH_A + CH_B, 2 * D_MODEL, "sigmoid", BF16)

    bias_slab = jnp.repeat(b_s[0].T, HEAD_DIM_A, axis=1).astype(F32)
    wexp = jnp.repeat(w_s[0][:, :DEC_SEQ, :DEC_SEQ].transpose(1, 2, 0).reshape(DEC_SEQ * DEC_SEQ, HEADS_A),
                      HEAD_DIM_A, axis=1).astype(F32)
    lvg, lvb = row2d(ln_v_g[0]), row2d(ln_v_b[0])
    oa_p = _spatial_prompt(ug, w_s[0], bias_slab, lvg, lvb)
    oa_s, vn_s = _spatial_sample(ug, wexp, bias_slab, lvg, lvb)
    o_a = jnp.concatenate([oa_p, oa_s], axis=0)

    w_pool_b = w_pool[0].astype(BF16)
    pscale = row2d(pool_scale[0])
    buf_t = state_pool[0].transpose(1, 0, 2)
    ob_p = _pool_prompt(z, w_pool_b, pscale)
    ob_s = _pool_sample(z, buf_t, w_pool_b, pscale)
    o_b = jnp.concatenate([ob_p, ob_s], axis=0)

    m = _merge(o_a, o_b, w_up_a[0].astype(BF16), w_up_b[0].astype(BF16), gates)
    r1 = _out_proj(m, w_o[0].astype(BF16), x_prompt_rows, x_sample_rows)
    x1, x1b, x1pk, e_pad, gw_pad, rank_pad, counts = _ln_router(
        r1, row2d(ln_g[0, 0]), row2d(ln_b[0, 0]), w_router[0].astype(BF16), row2d(b_router[0]))

    item_e, item_start, item_n, sdst = _dispatch(e_pad, rank_pad, counts[0].astype(jnp.int32))
    y = _moe_routed(x1pk.reshape(T_ALL * PK_ROWS, LANES), w1[0], w3[0], w2[0], item_e, item_start, item_n, sdst)
    base = _shared_expert(x1, x1b, ws1[0].astype(BF16), ws3[0].astype(BF16), ws2[0].astype(BF16))
    gwb = jnp.broadcast_to(gw_pad[:, :TOP_K].reshape(T_ALL, TOP_K, 1), (T_ALL, TOP_K, LANES))
    slab = lambda v: v.reshape(1, SLAB_ROWS, LANES).astype(F32)
    x2, x2b = _combine(y.reshape(T_ALL, TOP_K, PK_ROWS, LANES), gwb, base, slab(ln_g[0, 1]), slab(ln_b[0, 1]))

    r3 = _ple(x2b, w_ple_gate[0].astype(BF16), pb, w_ple_proj[0].astype(BF16), x2)
    y_p, y_s = _ln_split(r3, row2d(ln_g[0, 2]), row2d(ln_b[0, 2]))

    y_prompt = y_p.reshape(BATCH, SEQ, D_MODEL)
    y_sample = y_s.reshape(DEC_SEQ, DEC_BATCH, D_MODEL).transpose(1, 0, 2)
    z_s = z[T_PROMPT:].reshape(DEC_SEQ, DEC_BATCH, CH_B).transpose(1, 0, 2)
    new_pool_prompt = jnp.stack([z[(b + 1) * SEQ - POOL_BUF:(b + 1) * SEQ] for b in range(BATCH)])[None]
    new_pool_sample = jnp.concatenate([state_pool[0][:, DEC_SEQ:], z_s], axis=1)[None]
    new_chunk_v_sample = vn_s.reshape(DEC_SEQ, DEC_BATCH, CH_A).transpose(1, 0, 2)[None]
    return (y_prompt, y_sample, new_pool_prompt, new_pool_sample, new_chunk_v_sample)
```

```python
import functools

import jax
import jax.numpy as jnp
from jax import lax
from jax.experimental import pallas as pl
from jax.experimental.pallas import tpu as pltpu

F32 = jnp.float32
BF16 = jnp.bfloat16

D_MODEL = 4096
BATCH = 4
SEQ = 2048
DEC_BATCH = 128
DEC_SEQ = 4
PAST_LEN = 16384
CHUNK = 128
HEAD_DIM_A = 128
HEADS_A = 16
CH_A = 2048
POOL_WINDOWS = (2, 4, 8, 16)
POOL_GROUPS = 4
CH_B = 2048
POOL_GC = 512
POOL_BUF = 15
N_EXPERTS = 256
TOP_K = 8
N_GROUPS = 8
TOPK_GROUPS = 4
D_EXPERT = 512
D_SHARED = 512
ROUTE_SCALE = 2.5
PLE_DIM = 256
ALPHA = 2.0 ** 0.25
LN_EPS = 1e-5

T_PROMPT = BATCH * SEQ
T_SAMPLE = DEC_BATCH * DEC_SEQ
T_ALL = T_PROMPT + T_SAMPLE
N_ASSIGN = T_ALL * TOP_K

LANES = 128
SLAB_ROWS = D_MODEL // LANES
SLAB_STRIDE = 40
VMEM_LIMIT = 56 * 1024 * 1024

TM = 1088
TN = 512
TM_OUT = 512
TM_LN = 256
TM_SHARED = 256
TM_COMBINE = 64
POOL_TM = 512
POOL_HALO = 16
MOE_ROWS = 384
MOE_BUCKET = 64
MOE_SPLIT = 4
MOE_WBUF = 3
MOE_DQ = D_EXPERT // MOE_SPLIT
MOE_KQ = D_MODEL // MOE_SPLIT
MOE_KSUB = 512
MOE_DMA_PARTS = 1
MOE_ITEMS = N_EXPERTS + N_ASSIGN // MOE_ROWS
HALF = D_MODEL // 2
PK_ROWS = HALF // LANES
PK_STRIDE = 24
MOE_NCHUNK = 512
MOE_DMA_UNROLL = 8
MOE_WAIT_GROUP = 16
TOP_K_SHIFT = 3
HI_MASK = 0xFFFF0000


def _pack_pair(hi, lo):
    h = lax.bitcast_convert_type(hi.astype(BF16).astype(F32), jnp.uint32)
    l = lax.bitcast_convert_type(lo.astype(BF16).astype(F32), jnp.uint32)
    return h | (l >> jnp.uint32(16))


def _unpack_pair(w):
    hi = lax.bitcast_convert_type(w & jnp.uint32(HI_MASK), F32)
    lo = lax.bitcast_convert_type(w << jnp.uint32(16), F32)
    return hi, lo


def _cparams(sem):
    return pltpu.CompilerParams(dimension_semantics=sem, vmem_limit_bytes=VMEM_LIMIT)


def _layer_norm(xf, g, b):
    mu = jnp.mean(xf, axis=-1, keepdims=True)
    xc = xf - mu
    var = jnp.mean(xc * xc, axis=-1, keepdims=True)
    return xc * lax.rsqrt(var + LN_EPS) * g + b


def _proj_kernel(x_ref, w_ref, o_ref, *, act):
    h = jnp.dot(x_ref[...], w_ref[...], preferred_element_type=F32)
    if act == "gelu":
        h = jax.nn.gelu(h, approximate=True)
    elif act == "sigmoid":
        h = jax.nn.sigmoid(h)
    o_ref[...] = h.astype(o_ref.dtype)


def _proj(x, w, col0, ncols, act, out_dtype):
    t, k = x.shape
    off = col0 // TN
    return pl.pallas_call(
        functools.partial(_proj_kernel, act=act),
        grid=(t // TM, ncols // TN),
        in_specs=[pl.BlockSpec((TM, k), lambda m, n: (m, 0)),
                  pl.BlockSpec((k, TN), lambda m, n: (0, n + off))],
        out_specs=pl.BlockSpec((TM, TN), lambda m, n: (m, n)),
        out_shape=jax.ShapeDtypeStruct((t, ncols), out_dtype),
        compiler_params=_cparams(("parallel", "arbitrary")),
        name="in_proj_" + act,
    )(x, w)


def _spatial_prompt_kernel(u_ref, gv_ref, ws_ref, bias_ref, g_ref, b_ref, o_ref):
    vn = _layer_norm(gv_ref[...].astype(F32), g_ref[...], b_ref[...])
    row = lax.broadcasted_iota(jnp.int32, (CHUNK, CHUNK), 0)
    col = lax.broadcasted_iota(jnp.int32, (CHUNK, CHUNK), 1)
    causal = col <= row
    for h in range(HEADS_A):
        sl = slice(h * HEAD_DIM_A, (h + 1) * HEAD_DIM_A)
        w = jnp.where(causal, ws_ref[h], 0.0).astype(BF16)
        s = jnp.dot(w, vn[:, sl].astype(BF16), preferred_element_type=F32) + bias_ref[:, sl]
        o_ref[:, sl] = (u_ref[:, sl].astype(F32) * s).astype(o_ref.dtype)


def _spatial_prompt(ug, w_s, bias_slab, ln_g, ln_b):
    n_chunks = T_PROMPT // CHUNK
    return pl.pallas_call(
        _spatial_prompt_kernel,
        grid=(n_chunks,),
        in_specs=[pl.BlockSpec((CHUNK, CH_A), lambda c: (c, 0)),
                  pl.BlockSpec((CHUNK, CH_A), lambda c: (c, 1)),
                  pl.BlockSpec((HEADS_A, CHUNK, CHUNK), lambda c: (0, 0, 0)),
                  pl.BlockSpec((CHUNK, CH_A), lambda c: (0, 0)),
                  pl.BlockSpec((1, CH_A), lambda c: (0, 0)),
                  pl.BlockSpec((1, CH_A), lambda c: (0, 0))],
        out_specs=pl.BlockSpec((CHUNK, CH_A), lambda c: (c, 0)),
        out_shape=jax.ShapeDtypeStruct((T_PROMPT, CH_A), BF16),
        compiler_params=_cparams(("parallel",)),
        name="spatial_prompt",
    )(ug, ug, w_s, bias_slab, ln_g, ln_b)


def _spatial_sample_kernel(u_ref, gv_ref, wexp_ref, bias_ref, g_ref, b_ref, o_ref, vn_ref):
    vn_ref[...] = _layer_norm(gv_ref[...].astype(F32), g_ref[...], b_ref[...])
    for t in range(DEC_SEQ):
        rows = slice(t * DEC_BATCH, (t + 1) * DEC_BATCH)
        s = jnp.broadcast_to(bias_ref[t:t + 1, :], (DEC_BATCH, CH_A))
        for j in range(t + 1):
            r = t * DEC_SEQ + j
            s = s + wexp_ref[r:r + 1, :] * vn_ref[j * DEC_BATCH:(j + 1) * DEC_BATCH, :]
        o_ref[rows, :] = (u_ref[rows, :].astype(F32) * s).astype(o_ref.dtype)


def _spatial_sample(ug, wexp, bias_slab, ln_g, ln_b):
    blk = T_PROMPT // T_SAMPLE
    return pl.pallas_call(
        _spatial_sample_kernel,
        grid=(1,),
        in_specs=[pl.BlockSpec((T_SAMPLE, CH_A), lambda i: (blk, 0)),
                  pl.BlockSpec((T_SAMPLE, CH_A), lambda i: (blk, 1)),
                  pl.BlockSpec((DEC_SEQ * DEC_SEQ, CH_A), lambda i: (0, 0)),
                  pl.BlockSpec((CHUNK, CH_A), lambda i: (0, 0)),
                  pl.BlockSpec((1, CH_A), lambda i: (0, 0)),
                  pl.BlockSpec((1, CH_A), lambda i: (0, 0))],
        out_specs=[pl.BlockSpec((T_SAMPLE, CH_A), lambda i: (0, 0)),
                   pl.BlockSpec((T_SAMPLE, CH_A), lambda i: (0, 0))],
        out_shape=[jax.ShapeDtypeStruct((T_SAMPLE, CH_A), BF16),
                   jax.ShapeDtypeStruct((T_SAMPLE, CH_A), F32)],
        compiler_params=_cparams(("arbitrary",)),
        name="spatial_sample",
    )(ug, ug, wexp, bias_slab, ln_g, ln_b)


def _pool_project(d_of_group, wp_ref, scale_ref, o_ref):
    for g in range(POOL_GROUPS):
        sl = slice(g * POOL_GC, (g + 1) * POOL_GC)
        y = jnp.dot(d_of_group(g).astype(BF16), wp_ref[g], preferred_element_type=F32)
        o_ref[:, sl] = (y * scale_ref[:, sl]).astype(o_ref.dtype)


def _pool_prompt_kernel(z_ref, halo_ref, wp_ref, scale_ref, o_ref, cat_ref):
    i = pl.program_id(1)
    halo = halo_ref[...]
    cat_ref[0:POOL_HALO, :] = jnp.where(i == 0, jnp.zeros_like(halo), halo)
    cat_ref[POOL_HALO:, :] = z_ref[...]
    pos = i * POOL_TM + lax.broadcasted_iota(jnp.int32, (POOL_TM, POOL_GC), 0)

    def d_of_group(g):
        win = POOL_WINDOWS[g]
        sl = slice(g * POOL_GC, (g + 1) * POOL_GC)
        acc = cat_ref[POOL_HALO:POOL_HALO + POOL_TM, sl]
        for k in range(1, win):
            acc = acc + cat_ref[POOL_HALO - k:POOL_HALO - k + POOL_TM, sl]
        cnt = jnp.minimum(pos + 1, win).astype(F32)
        return acc / cnt - z_ref[:, sl]

    _pool_project(d_of_group, wp_ref, scale_ref, o_ref)


def _pool_prompt(z, w_pool, pool_scale):
    tiles = SEQ // POOL_TM
    per_tile = POOL_TM // POOL_HALO
    return pl.pallas_call(
        _pool_prompt_kernel,
        grid=(BATCH, tiles),
        in_specs=[pl.BlockSpec((POOL_TM, CH_B), lambda b, i: (b * tiles + i, 0)),
                  pl.BlockSpec((POOL_HALO, CH_B),
                               lambda b, i: (jnp.maximum((b * tiles + i) * per_tile - 1, 0), 0)),
                  pl.BlockSpec((POOL_GROUPS, POOL_GC, POOL_GC), lambda b, i: (0, 0, 0)),
                  pl.BlockSpec((1, CH_B), lambda b, i: (0, 0))],
        out_specs=pl.BlockSpec((POOL_TM, CH_B), lambda b, i: (b * tiles + i, 0)),
        out_shape=jax.ShapeDtypeStruct((T_PROMPT, CH_B), BF16),
        scratch_shapes=[pltpu.VMEM((POOL_HALO + POOL_TM, CH_B), F32)],
        compiler_params=_cparams(("parallel", "arbitrary")),
        name="pool_prompt",
    )(z, z, w_pool, pool_scale)


def _pool_sample_kernel(z_ref, buf_ref, wp_ref, scale_ref, o_ref, d_ref):
    def cat_row(r, sl):
        if r < POOL_BUF:
            return buf_ref[r, :, sl]
        t = r - POOL_BUF
        return z_ref[t * DEC_BATCH:(t + 1) * DEC_BATCH, sl]

    for g in range(POOL_GROUPS):
        win = POOL_WINDOWS[g]
        sl = slice(g * POOL_GC, (g + 1) * POOL_GC)
        for t in range(DEC_SEQ):
            acc = cat_row(POOL_BUF + t, sl)
            for k in range(1, win):
                acc = acc + cat_row(POOL_BUF + t - k, sl)
            cnt = float(min(PAST_LEN + t + 1, win))
            d_ref[t * DEC_BATCH:(t + 1) * DEC_BATCH, sl] = acc / cnt - cat_row(POOL_BUF + t, sl)

    _pool_project(lambda g: d_ref[:, g * POOL_GC:(g + 1) * POOL_GC], wp_ref, scale_ref, o_ref)


def _pool_sample(z, buf_t, w_pool, pool_scale):
    blk = T_PROMPT // T_SAMPLE
    return pl.pallas_call(
        _pool_sample_kernel,
        grid=(1,),
        in_specs=[pl.BlockSpec((T_SAMPLE, CH_B), lambda i: (blk, 0)),
                  pl.BlockSpec((POOL_BUF, DEC_BATCH, CH_B), lambda i: (0, 0, 0)),
                  pl.BlockSpec((POOL_GROUPS, POOL_GC, POOL_GC), lambda i: (0, 0, 0)),
                  pl.BlockSpec((1, CH_B), lambda i: (0, 0))],
        out_specs=pl.BlockSpec((T_SAMPLE, CH_B), lambda i: (0, 0)),
        out_shape=jax.ShapeDtypeStruct((T_SAMPLE, CH_B), BF16),
        scratch_shapes=[pltpu.VMEM((T_SAMPLE, CH_B), F32)],
        compiler_params=_cparams(("arbitrary",)),
        name="pool_sample",
    )(z, buf_t, w_pool, pool_scale)


def _merge_kernel(oa_ref, ob_ref, wa_ref, wb_ref, ga_ref, gb_ref, o_ref):
    a = jnp.dot(oa_ref[...], wa_ref[...], preferred_element_type=F32)
    b = jnp.dot(ob_ref[...], wb_ref[...], preferred_element_type=F32)
    o_ref[...] = (ga_ref[...].astype(F32) * a + gb_ref[...].astype(F32) * b).astype(o_ref.dtype)


def _merge(o_a, o_b, w_up_a, w_up_b, gates):
    nb = D_MODEL // TN
    return pl.pallas_call(
        _merge_kernel,
        grid=(T_ALL // TM, nb),
        in_specs=[pl.BlockSpec((TM, CH_A), lambda m, n: (m, 0)),
                  pl.BlockSpec((TM, CH_B), lambda m, n: (m, 0)),
                  pl.BlockSpec((CH_A, TN), lambda m, n: (0, n)),
                  pl.BlockSpec((CH_B, TN), lambda m, n: (0, n)),
                  pl.BlockSpec((TM, TN), lambda m, n: (m, n)),
                  pl.BlockSpec((TM, TN), lambda m, n: (m, n + nb))],
        out_specs=pl.BlockSpec((TM, TN), lambda m, n: (m, n)),
        out_shape=jax.ShapeDtypeStruct((T_ALL, D_MODEL), BF16),
        compiler_params=_cparams(("parallel", "arbitrary")),
        name="merge",
    )(o_a, o_b, w_up_a, w_up_b, gates, gates)


def _out_proj_kernel(m_ref, w_ref, xp_ref, xs_ref, o_ref):
    x = jnp.where(pl.program_id(0) < T_PROMPT // TM_OUT, xp_ref[...], xs_ref[...])
    o_ref[...] = ALPHA * x + jnp.dot(m_ref[...], w_ref[...], preferred_element_type=F32)


def _out_proj(m, w_o, x_prompt, x_sample):
    n_prompt = T_PROMPT // TM_OUT
    return pl.pallas_call(
        _out_proj_kernel,
        grid=(T_ALL // TM_OUT, D_MODEL // TN),
        in_specs=[pl.BlockSpec((TM_OUT, D_MODEL), lambda m_, n: (m_, 0)),
                  pl.BlockSpec((D_MODEL, TN), lambda m_, n: (0, n)),
                  pl.BlockSpec((TM_OUT, TN), lambda m_, n: (jnp.minimum(m_, n_prompt - 1), n)),
                  pl.BlockSpec((TM_OUT, TN), lambda m_, n: (0, n))],
        out_specs=pl.BlockSpec((TM_OUT, TN), lambda m_, n: (m_, n)),
        out_shape=jax.ShapeDtypeStruct((T_ALL, D_MODEL), F32),
        compiler_params=_cparams(("parallel", "arbitrary")),
        name="out_proj",
    )(m, w_o, x_prompt, x_sample)


def _route_tile(scores, bias, carry):
    tm = scores.shape[0]
    lane = lax.broadcasted_iota(jnp.int32, (tm, N_EXPERTS), 1)
    per_group = N_EXPERTS // N_GROUPS
    neg = -jnp.inf
    biased = scores + bias

    def first_argmax(v):
        m = jnp.max(v, axis=-1, keepdims=True)
        return m, jnp.min(jnp.where(v == m, lane, N_EXPERTS), axis=-1, keepdims=True)

    in_group = [jnp.logical_and(lane >= g * per_group, lane < (g + 1) * per_group) for g in range(N_GROUPS)]
    gscore = []
    for g in range(N_GROUPS):
        v = jnp.where(in_group[g], biased, neg)
        m1, i1 = first_argmax(v)
        m2 = jnp.max(jnp.where(lane == i1, neg, v), axis=-1, keepdims=True)
        gscore.append(m1 + m2)
    group_kept = jnp.zeros((tm, N_EXPERTS), jnp.int32)
    for g in range(N_GROUPS):
        beaten_by = jnp.zeros((tm, 1), jnp.int32)
        for h in range(N_GROUPS):
            if h == g:
                continue
            wins = gscore[h] >= gscore[g] if h < g else gscore[h] > gscore[g]
            beaten_by = beaten_by + jnp.where(wins, 1, 0)
        kept = jnp.where(beaten_by < TOPK_GROUPS, 1, 0)
        group_kept = group_kept + jnp.where(in_group[g], kept, 0)
    masked = jnp.where(group_kept > 0, biased, neg)

    idx, gate = [], []
    sel_f = jnp.zeros((tm, N_EXPERTS), F32)
    for _ in range(TOP_K):
        _, i = first_argmax(masked)
        hit = lane == i
        idx.append(i)
        gate.append(jnp.sum(jnp.where(hit, scores, 0.0), axis=-1, keepdims=True))
        sel_f = sel_f + jnp.where(hit, 1.0, 0.0)
        masked = jnp.where(hit, neg, masked)
    total = gate[0]
    for k in range(1, TOP_K):
        total = total + gate[k]
    gate = [gk / total * ROUTE_SCALE for gk in gate]

    r_i = lax.broadcasted_iota(jnp.int32, (tm, tm), 0)
    c_i = lax.broadcasted_iota(jnp.int32, (tm, tm), 1)
    lower = jnp.where(c_i < r_i, 1.0, 0.0).astype(BF16)
    rank_full = carry + jnp.dot(lower, sel_f.astype(BF16), preferred_element_type=F32)
    rank = [jnp.sum(jnp.where(lane == i, rank_full, 0.0), axis=-1, keepdims=True) for i in idx]
    return idx, gate, rank, jnp.sum(sel_f, axis=0, keepdims=True)


def _columns_to_lanes(cols, dtype):
    tm = cols[0].shape[0]
    lane = lax.broadcasted_iota(jnp.int32, (tm, LANES), 1)
    out = jnp.zeros((tm, LANES), dtype)
    for k, c in enumerate(cols):
        out = jnp.where(lane == k, c.astype(dtype), out)
    return out


def _ln_router_kernel(r_ref, g_ref, b_ref, wr_ref, br_ref, x_ref, xb_ref, xpk_ref, e_ref, gw_ref, rk_ref, cnt_ref):
    @pl.when(pl.program_id(0) == 0)
    def _init():
        cnt_ref[...] = jnp.zeros_like(cnt_ref)

    x1 = _layer_norm(r_ref[...], g_ref[...], b_ref[...])
    x_ref[...] = x1
    xb = x1.astype(BF16)
    xb_ref[...] = xb
    xpk_ref[...] = _pack_pair(x1[:, :HALF], x1[:, HALF:])
    scores = jax.nn.sigmoid(jnp.dot(xb, wr_ref[...], preferred_element_type=F32))
    idx, gate, rank, counts = _route_tile(scores, br_ref[...], cnt_ref[...])
    e_ref[...] = _columns_to_lanes(idx, jnp.int32)
    gw_ref[...] = _columns_to_lanes(gate, F32)
    rk_ref[...] = _columns_to_lanes(rank, jnp.int32)
    cnt_ref[...] = cnt_ref[...] + counts


def _ln_router(r, g, b, w_router, b_router):
    row = lambda i: (i, 0)
    fixed = lambda i: (0, 0)
    return pl.pallas_call(
        _ln_router_kernel,
        grid=(T_ALL // TM_LN,),
        in_specs=[pl.BlockSpec((TM_LN, D_MODEL), row),
                  pl.BlockSpec((1, D_MODEL), fixed),
                  pl.BlockSpec((1, D_MODEL), fixed),
                  pl.BlockSpec((D_MODEL, N_EXPERTS), fixed),
                  pl.BlockSpec((1, N_EXPERTS), fixed)],
        out_specs=[pl.BlockSpec((TM_LN, D_MODEL), row),
                   pl.BlockSpec((TM_LN, D_MODEL), row),
                   pl.BlockSpec((TM_LN, HALF), row),
                   pl.BlockSpec((TM_LN, LANES), row),
                   pl.BlockSpec((TM_LN, LANES), row),
                   pl.BlockSpec((TM_LN, LANES), row),
                   pl.BlockSpec((1, N_EXPERTS), fixed)],
        out_shape=[jax.ShapeDtypeStruct((T_ALL, D_MODEL), F32),
                   jax.ShapeDtypeStruct((T_ALL, D_MODEL), BF16),
                   jax.ShapeDtypeStruct((T_ALL, HALF), jnp.uint32),
                   jax.ShapeDtypeStruct((T_ALL, LANES), jnp.int32),
                   jax.ShapeDtypeStruct((T_ALL, LANES), F32),
                   jax.ShapeDtypeStruct((T_ALL, LANES), jnp.int32),
                   jax.ShapeDtypeStruct((1, N_EXPERTS), F32)],
        compiler_params=_cparams(("arbitrary",)),
        name="ln_router",
    )(r, g, b, w_router, b_router)


def _ln_split_kernel(r_ref, g_ref, b_ref, op_ref, os_ref):
    i = pl.program_id(0)
    y = _layer_norm(r_ref[...], g_ref[...], b_ref[...])

    @pl.when(i < T_PROMPT // TM_LN)
    def _prompt():
        op_ref[...] = y

    @pl.when(i >= T_PROMPT // TM_LN)
    def _sample():
        os_ref[...] = y


def _ln_split(r, g, b):
    fixed = lambda i: (0, 0)
    n_prompt = T_PROMPT // TM_LN
    return pl.pallas_call(
        _ln_split_kernel,
        grid=(T_ALL // TM_LN,),
        in_specs=[pl.BlockSpec((TM_LN, D_MODEL), lambda i: (i, 0)),
                  pl.BlockSpec((1, D_MODEL), fixed),
                  pl.BlockSpec((1, D_MODEL), fixed)],
        out_specs=[pl.BlockSpec((TM_LN, D_MODEL), lambda i: (jnp.minimum(i, n_prompt - 1), 0)),
                   pl.BlockSpec((TM_LN, D_MODEL), lambda i: (jnp.maximum(i - n_prompt, 0), 0))],
        out_shape=[jax.ShapeDtypeStruct((T_PROMPT, D_MODEL), F32),
                   jax.ShapeDtypeStruct((T_SAMPLE, D_MODEL), F32)],
        compiler_params=_cparams(("arbitrary",)),
        name="ln_final",
    )(r, g, b)


def _moe_kernel(item_e, item_start, item_n, sdst, x_hbm, w1_hbm, w3_hbm, w2_hbm, y_hbm,
                xslab, xb, w13b, w2b, hacc, yslab, gsem, ssem, wbuf1, wbuf3, wbuf2, wsem):
    g = pl.program_id(0) * MOE_SPLIT + pl.program_id(1)
    wslot = g % MOE_WBUF

    def weight_copies(item, qq, slot_):
        e = item_e[item]
        k0 = pl.multiple_of(qq * MOE_KQ, MOE_KQ)
        d0 = pl.multiple_of(qq * MOE_DQ, MOE_DQ)
        return (pltpu.make_async_copy(w1_hbm.at[e, pl.ds(k0, MOE_KQ), :], wbuf1.at[slot_], wsem.at[slot_]),
                pltpu.make_async_copy(w3_hbm.at[e, pl.ds(k0, MOE_KQ), :], wbuf3.at[slot_], wsem.at[slot_]),
                pltpu.make_async_copy(w2_hbm.at[e, pl.ds(d0, MOE_DQ), :], wbuf2.at[slot_], wsem.at[slot_]))

    def start_weights(gg):
        item = jnp.minimum(gg // MOE_SPLIT, MOE_ITEMS - 1)

        @pl.when(jnp.logical_and(gg < MOE_ITEMS * MOE_SPLIT, item_n[item] > 0))
        def _():
            for c in weight_copies(item, gg % MOE_SPLIT, gg % MOE_WBUF):
                c.start()

    @pl.when(g == 0)
    def _prime():
        for gg in range(MOE_WBUF - 1):
            start_weights(jnp.int32(gg))

    start_weights(g + MOE_WBUF - 1)

    @pl.when(item_n[pl.program_id(0)] > 0)
    def _wait_weights():
        for c in weight_copies(pl.program_id(0), pl.program_id(1), wslot):
            c.wait()

    w1_refs, w3_refs, w2_refs = [wbuf1.at[wslot]], [wbuf3.at[wslot]], [wbuf2.at[wslot]]
    _moe_body(item_start, item_n, sdst, x_hbm, w1_refs, w3_refs, w2_refs, y_hbm,
              xslab, xb, w13b, w2b, hacc, yslab, gsem, ssem)


def _moe_body(item_start, item_n, sdst, x_hbm, w1_refs, w3_refs, w2_refs, y_hbm,
              xslab, xb, w13b, w2b, hacc, yslab, gsem, ssem):
    w = pl.program_id(0)
    q = pl.program_id(1)
    n = item_n[w]
    start = item_start[w]
    slot = w % 2
    last_q = MOE_SPLIT - 1

    def x_copy(tok, r, slot_):
        return pltpu.make_async_copy(x_hbm.at[pl.ds(tok * PK_ROWS, PK_ROWS), :],
                                     xslab.at[slot_, pl.ds(r * PK_STRIDE, PK_ROWS), :],
                                     gsem.at[slot_])

    def y_copy(r, dst):
        return pltpu.make_async_copy(yslab.at[pl.ds(r * PK_STRIDE, PK_ROWS), :],
                                     y_hbm.at[pl.ds(dst * PK_ROWS, PK_ROWS), :], ssem)

    def for_each_row(count, row_fn):
        groups = count // MOE_DMA_UNROLL

        def group_body(g, c):
            for u in range(MOE_DMA_UNROLL):
                row_fn(g * MOE_DMA_UNROLL + u)
            return c

        def tail_body(r, c):
            row_fn(r)
            return c

        lax.fori_loop(0, groups, group_body, 0)
        lax.fori_loop(groups * MOE_DMA_UNROLL, count, tail_body, 0)

    def start_gather(item, slot_):
        base = item_start[item]
        for_each_row(item_n[item],
                     lambda r: x_copy(lax.shift_right_logical(sdst[base + r], TOP_K_SHIFT), r, slot_).start())

    def wait_rows(count, group_wait, row_wait):
        groups = count // MOE_WAIT_GROUP

        def group_body(g, c):
            group_wait()
            return c

        def tail_body(r, c):
            row_wait(r)
            return c

        lax.fori_loop(0, groups, group_body, 0)
        lax.fori_loop(groups * MOE_WAIT_GROUP, count, tail_body, 0)

    group_rows = MOE_WAIT_GROUP * PK_ROWS

    def wait_gather(count, slot_):
        wait_rows(count,
                  lambda: pltpu.make_async_copy(x_hbm.at[pl.ds(0, group_rows), :],
                                                xslab.at[slot_, pl.ds(0, group_rows), :], gsem.at[slot_]).wait(),
                  lambda r: x_copy(0, r, slot_).wait())

    def wait_scatter(count):
        wait_rows(count,
                  lambda: pltpu.make_async_copy(yslab.at[pl.ds(0, group_rows), :],
                                                y_hbm.at[pl.ds(0, group_rows), :], ssem).wait(),
                  lambda r: y_copy(r, 0).wait())

    @pl.when(jnp.logical_and(w == 0, q == 0))
    def _first():
        xslab[...] = jnp.zeros_like(xslab)
        yslab[...] = jnp.zeros_like(yslab)
        hacc[...] = jnp.zeros_like(hacc)
        start_gather(0, 0)

    @pl.when(jnp.logical_and(q == 0, n > 0))
    def _wait_rows():
        wait_gather(n, slot)

    @pl.when(jnp.logical_and(q == 1, w + 1 < MOE_ITEMS))
    def _prefetch_rows():
        start_gather(w + 1, 1 - slot)

    @pl.when(jnp.logical_and(q == last_q, w > 0))
    def _wait_prev_scatter():
        wait_scatter(item_n[w - 1])

    @pl.when(n > 0)
    def _compute():
        nb = (n + MOE_BUCKET - 1) // MOE_BUCKET
        for b in range(1, MOE_ROWS // MOE_BUCKET + 1):
            rows = b * MOE_BUCKET

            @pl.when(nb == b)
            def _bucket(rows=rows):
                @pl.when(q == 0)
                def _unpack():
                    for s in range(PK_ROWS):
                        hi, lo = _unpack_pair(xslab[slot, pl.ds(s, rows, stride=PK_STRIDE), :])
                        for col, val in ((s * LANES, hi), (HALF + s * LANES, lo)):
                            xb[col // MOE_KQ, 0:rows, col % MOE_KQ:col % MOE_KQ + LANES] = val.astype(BF16)

                part = None
                kp, dp = MOE_KQ // MOE_DMA_PARTS, MOE_DQ // MOE_DMA_PARTS
                piece = min(kp, MOE_KSUB)
                for c in range(MOE_KQ // MOE_KSUB):
                    ks = slice(c * MOE_KSUB, (c + 1) * MOE_KSUB)
                    for r0 in range(c * MOE_KSUB, (c + 1) * MOE_KSUB, piece):
                        j, off = r0 // kp, r0 % kp
                        w13b[r0:r0 + piece, :D_EXPERT] = w1_refs[j][off:off + piece, :].astype(BF16)
                        w13b[r0:r0 + piece, D_EXPERT:] = w3_refs[j][off:off + piece, :].astype(BF16)
                    d = jnp.dot(xb[q, 0:rows, ks], w13b[ks, :], preferred_element_type=F32)
                    part = d if part is None else part + d
                h = jnp.where(q == 0, part, hacc[0:rows, :] + part)
                hacc[0:rows, :] = h
                for j in range(MOE_DMA_PARTS):
                    w2b[pl.ds(pl.multiple_of(q * MOE_DQ + j * dp, dp), dp), :] = w2_refs[j][...].astype(BF16)

                @pl.when(q == last_q)
                def _down():
                    a = (jax.nn.silu(h[:, :D_EXPERT]) * h[:, D_EXPERT:]).astype(BF16)
                    per_chunk = MOE_NCHUNK // LANES
                    for c in range(HALF // MOE_NCHUNK):
                        lo_cols = slice(c * MOE_NCHUNK, (c + 1) * MOE_NCHUNK)
                        hi_cols = slice(HALF + c * MOE_NCHUNK, HALF + (c + 1) * MOE_NCHUNK)
                        pk = _pack_pair(jnp.dot(a, w2b[:, lo_cols], preferred_element_type=F32),
                                        jnp.dot(a, w2b[:, hi_cols], preferred_element_type=F32))
                        for s in range(per_chunk):
                            yslab[pl.ds(c * per_chunk + s, rows, stride=PK_STRIDE), :] = pk[:, s * LANES:(s + 1) * LANES]

    @pl.when(jnp.logical_and(q == last_q, n > 0))
    def _scatter():
        for_each_row(n, lambda r: y_copy(r, sdst[start + r]).start())

    @pl.when(jnp.logical_and(jnp.logical_and(q == last_q, w == MOE_ITEMS - 1), n > 0))
    def _drain():
        wait_scatter(n)


def _moe_routed(x_pk, w1, w3, w2, item_e, item_start, item_n, sdst):
    grid_spec = pltpu.PrefetchScalarGridSpec(
        num_scalar_prefetch=4,
        grid=(MOE_ITEMS, MOE_SPLIT),
        in_specs=[pl.BlockSpec(memory_space=pl.ANY)] * 4,
        out_specs=pl.BlockSpec(memory_space=pl.ANY),
        scratch_shapes=[
            pltpu.VMEM((2, MOE_ROWS * PK_STRIDE, LANES), jnp.uint32),
            pltpu.VMEM((MOE_SPLIT, MOE_ROWS, MOE_KQ), BF16),
            pltpu.VMEM((MOE_KQ, 2 * D_EXPERT), BF16),
            pltpu.VMEM((D_EXPERT, D_MODEL), BF16),
            pltpu.VMEM((MOE_ROWS, 2 * D_EXPERT), F32),
            pltpu.VMEM((MOE_ROWS * PK_STRIDE, LANES), jnp.uint32),
            pltpu.SemaphoreType.DMA((2,)),
            pltpu.SemaphoreType.DMA(()),
            pltpu.VMEM((MOE_WBUF, MOE_KQ, D_EXPERT), F32),
            pltpu.VMEM((MOE_WBUF, MOE_KQ, D_EXPERT), F32),
            pltpu.VMEM((MOE_WBUF, MOE_DQ, D_MODEL), F32),
            pltpu.SemaphoreType.DMA((MOE_WBUF,)),
        ],
    )
    return pl.pallas_call(
        _moe_kernel,
        grid_spec=grid_spec,
        out_shape=jax.ShapeDtypeStruct((N_ASSIGN * PK_ROWS, LANES), jnp.uint32),
        compiler_params=_cparams(("arbitrary", "arbitrary")),
        name="moe_routed",
    )(item_e, item_start, item_n, sdst, x_pk, w1, w3, w2)


def _shared_kernel(x_ref, xb_ref, w1_ref, w3_ref, w2_ref, o_ref):
    xb = xb_ref[...]
    a = jax.nn.silu(jnp.dot(xb, w1_ref[...], preferred_element_type=F32))
    a = (a * jnp.dot(xb, w3_ref[...], preferred_element_type=F32)).astype(BF16)
    o_ref[...] = ALPHA * x_ref[...] + jnp.dot(a, w2_ref[...], preferred_element_type=F32)


def _shared_expert(x, xb, ws1, ws3, ws2):
    fixed = lambda i: (0, 0)
    return pl.pallas_call(
        _shared_kernel,
        grid=(T_ALL // TM_SHARED,),
        in_specs=[pl.BlockSpec((TM_SHARED, D_MODEL), lambda i: (i, 0)),
                  pl.BlockSpec((TM_SHARED, D_MODEL), lambda i: (i, 0)),
                  pl.BlockSpec((D_MODEL, D_SHARED), fixed),
                  pl.BlockSpec((D_MODEL, D_SHARED), fixed),
                  pl.BlockSpec((D_SHARED, D_MODEL), fixed)],
        out_specs=pl.BlockSpec((TM_SHARED, D_MODEL), lambda i: (i, 0)),
        out_shape=jax.ShapeDtypeStruct((T_ALL, D_MODEL), F32),
        compiler_params=_cparams(("parallel",)),
        name="shared_expert",
    )(x, xb, ws1, ws3, ws2)


def _combine_kernel(y_ref, gw_ref, base_ref, g_ref, b_ref, x_ref, xb_ref, slab_in, slab_out):
    tm = base_ref.shape[0]
    for s in range(SLAB_ROWS):
        slab_in[pl.ds(s, tm, stride=SLAB_STRIDE), :] = base_ref[:, s * LANES:(s + 1) * LANES]
    for s in range(SLAB_ROWS, SLAB_STRIDE):
        slab_in[pl.ds(s, tm, stride=SLAB_STRIDE), :] = jnp.zeros((tm, LANES), F32)
    base = slab_in[...].reshape(tm, SLAB_STRIDE, LANES)
    lo_rows, hi_rows = slice(0, PK_ROWS), slice(PK_ROWS, SLAB_ROWS)
    acc_lo = base[:, lo_rows, :]
    acc_hi = base[:, hi_rows, :]
    for k in range(TOP_K):
        y_lo, y_hi = _unpack_pair(y_ref[:, k])
        gate = gw_ref[:, k:k + 1, :]
        acc_lo = acc_lo + y_lo * gate
        acc_hi = acc_hi + y_hi * gate

    def token_sum(a, b):
        return jnp.sum(jnp.sum(a, axis=2, keepdims=True) + jnp.sum(b, axis=2, keepdims=True), axis=1, keepdims=True)

    inv_d = 1.0 / D_MODEL
    mu = token_sum(acc_lo, acc_hi) * inv_d
    c_lo, c_hi = acc_lo - mu, acc_hi - mu
    inv_std = lax.rsqrt(token_sum(c_lo * c_lo, c_hi * c_hi) * inv_d + LN_EPS)
    out_lo = c_lo * inv_std * g_ref[:, lo_rows, :] + b_ref[:, lo_rows, :]
    out_hi = c_hi * inv_std * g_ref[:, hi_rows, :] + b_ref[:, hi_rows, :]
    pad = jnp.zeros((tm, SLAB_STRIDE - SLAB_ROWS, LANES), F32)
    slab_out[...] = jnp.concatenate([out_lo, out_hi, pad], axis=1).reshape(tm * SLAB_STRIDE, LANES)
    for s in range(SLAB_ROWS):
        cols = slab_out[pl.ds(s, tm, stride=SLAB_STRIDE), :]
        x_ref[:, s * LANES:(s + 1) * LANES] = cols
        xb_ref[:, s * LANES:(s + 1) * LANES] = cols.astype(BF16)


def _combine(y, gwb, base, g_slab, b_slab):
    row = lambda i: (i, 0)
    return pl.pallas_call(
        _combine_kernel,
        grid=(T_ALL // TM_COMBINE,),
        in_specs=[pl.BlockSpec((TM_COMBINE, TOP_K, PK_ROWS, LANES), lambda i: (i, 0, 0, 0)),
                  pl.BlockSpec((TM_COMBINE, TOP_K, LANES), lambda i: (i, 0, 0)),
                  pl.BlockSpec((TM_COMBINE, D_MODEL), row),
                  pl.BlockSpec((1, SLAB_ROWS, LANES), lambda i: (0, 0, 0)),
                  pl.BlockSpec((1, SLAB_ROWS, LANES), lambda i: (0, 0, 0))],
        out_specs=[pl.BlockSpec((TM_COMBINE, D_MODEL), row),
                   pl.BlockSpec((TM_COMBINE, D_MODEL), row)],
        out_shape=[jax.ShapeDtypeStruct((T_ALL, D_MODEL), F32),
                   jax.ShapeDtypeStruct((T_ALL, D_MODEL), BF16)],
        scratch_shapes=[pltpu.VMEM((TM_COMBINE * SLAB_STRIDE, LANES), F32),
                        pltpu.VMEM((TM_COMBINE * SLAB_STRIDE, LANES), F32)],
        compiler_params=_cparams(("parallel",)),
        name="moe_combine_ln",
    )(y, gwb, base, g_slab, b_slab)


def _ple_kernel(xb_ref, wg_ref, p_ref, wp_ref, x_ref, o_ref):
    gate = jax.nn.sigmoid(jnp.dot(xb_ref[...], wg_ref[...], preferred_element_type=F32))
    proj = jnp.dot(p_ref[...], wp_ref[...], preferred_element_type=F32)
    o_ref[...] = ALPHA * x_ref[...] + gate * proj


def _ple(xb, w_gate, p, w_proj, x):
    return pl.pallas_call(
        _ple_kernel,
        grid=(T_ALL // TM, D_MODEL // TN),
        in_specs=[pl.BlockSpec((TM, D_MODEL), lambda m, n: (m, 0)),
                  pl.BlockSpec((D_MODEL, TN), lambda m, n: (0, n)),
                  pl.BlockSpec((TM, PLE_DIM), lambda m, n: (m, 0)),
                  pl.BlockSpec((PLE_DIM, TN), lambda m, n: (0, n)),
                  pl.BlockSpec((TM, TN), lambda m, n: (m, n))],
        out_specs=pl.BlockSpec((TM, TN), lambda m, n: (m, n)),
        out_shape=jax.ShapeDtypeStruct((T_ALL, D_MODEL), F32),
        compiler_params=_cparams(("parallel", "arbitrary")),
        name="ple",
    )(xb, w_gate, p, w_proj, x)


def _sorted_pos_kernel(e_ref, rk_ref, start_ref, o_ref):
    tm = e_ref.shape[0]
    lane = lax.broadcasted_iota(jnp.int32, (tm, N_EXPERTS), 1)
    cols = []
    for k in range(TOP_K):
        seg = jnp.sum(jnp.where(lane == e_ref[:, k:k + 1], start_ref[...], 0.0), axis=-1, keepdims=True)
        cols.append(seg.astype(jnp.int32) + rk_ref[:, k:k + 1])
    o_ref[...] = _columns_to_lanes(cols, jnp.int32)


def _sorted_pos(e_pad, rank_pad, seg_start):
    row = lambda i: (i, 0)
    return pl.pallas_call(
        _sorted_pos_kernel,
        grid=(T_ALL // TM,),
        in_specs=[pl.BlockSpec((TM, LANES), row),
                  pl.BlockSpec((TM, LANES), row),
                  pl.BlockSpec((1, N_EXPERTS), lambda i: (0, 0))],
        out_specs=pl.BlockSpec((TM, LANES), row),
        out_shape=jax.ShapeDtypeStruct((T_ALL, LANES), jnp.int32),
        compiler_params=_cparams(("parallel",)),
        name="sorted_pos",
    )(e_pad, rank_pad, seg_start)


def _invert_kernel(pos_ref, o_ref):
    def body(g, c):
        for u in range(MOE_DMA_UNROLL):
            i = g * MOE_DMA_UNROLL + u
            o_ref[pos_ref[i]] = i
        return c

    lax.fori_loop(0, N_ASSIGN // MOE_DMA_UNROLL, body, 0)


def _invert_permutation(pos):
    return pl.pallas_call(
        _invert_kernel,
        grid_spec=pltpu.PrefetchScalarGridSpec(
            num_scalar_prefetch=1, grid=(1,), in_specs=[],
            out_specs=pl.BlockSpec(memory_space=pltpu.SMEM)),
        out_shape=jax.ShapeDtypeStruct((N_ASSIGN,), jnp.int32),
        compiler_params=_cparams(("arbitrary",)),
        name="invert_permutation",
    )(pos)


def _dispatch(e_pad, rank_pad, counts):
    sstart = jnp.cumsum(counts) - counts
    pos = _sorted_pos(e_pad, rank_pad, sstart.astype(F32).reshape(1, N_EXPERTS))
    sdst = _invert_permutation(pos[:, :TOP_K].reshape(N_ASSIGN))
    n_items = (counts + MOE_ROWS - 1) // MOE_ROWS
    item_end = jnp.cumsum(n_items)
    total = item_end[-1]
    w = jnp.arange(MOE_ITEMS, dtype=jnp.int32)
    e_of = jnp.minimum(jnp.sum((item_end[None, :] <= w[:, None]).astype(jnp.int32), axis=1), N_EXPERTS - 1)
    local = w - (item_end - n_items)[e_of]
    valid = w < total
    e_last = e_of[jnp.maximum(total - 1, 0)]
    item_e = jnp.where(valid, e_of, e_last)
    item_start = jnp.where(valid, sstart[e_of] + local * MOE_ROWS, 0)
    item_n = jnp.where(valid, jnp.clip(counts[e_of] - local * MOE_ROWS, 0, MOE_ROWS), 0)
    return item_e.astype(jnp.int32), item_start.astype(jnp.int32), item_n.astype(jnp.int32), sdst


def kernel(x_prompt, x_sample, state_pool, p_prompt, p_sample, w_in, ln_v_g, ln_v_b, w_s, b_s, w_pool, pool_scale, w_up_a, w_up_b, w_o, ln_g, ln_b, w_router, b_router, w1, w3, w2, ws1, ws3, ws2, w_ple_gate, w_ple_proj):
    def tokens(a_prompt, a_sample):
        d = a_prompt.shape[-1]
        return jnp.concatenate([a_prompt.reshape(T_PROMPT, d).astype(BF16),
                                a_sample.transpose(1, 0, 2).reshape(T_SAMPLE, d).astype(BF16)], axis=0)

    xb = tokens(x_prompt, x_sample)
    pb = tokens(p_prompt[0], p_sample[0])
    x_prompt_rows = x_prompt.reshape(T_PROMPT, D_MODEL)
    x_sample_rows = x_sample.transpose(1, 0, 2).reshape(T_SAMPLE, D_MODEL)
    row2d = lambda v: v.reshape(1, -1).astype(F32)

    w_in_b = w_in[0].astype(BF16)
    ug = _proj(xb, w_in_b, 0, 2 * CH_A, "gelu", BF16)
    z = _proj(xb, w_in_b, 2 * CH_A, CH_B, "none", F32)
    gates = _proj(xb, w_in_b, 2 * CH_A + CH_B, 2 * D_MODEL, "sigmoid", BF16)

    bias_slab = jnp.repeat(b_s[0].T, HEAD_DIM_A, axis=1).astype(F32)
    wexp = jnp.repeat(w_s[0][:, :DEC_SEQ, :DEC_SEQ].transpose(1, 2, 0).reshape(DEC_SEQ * DEC_SEQ, HEADS_A),
                      HEAD_DIM_A, axis=1).astype(F32)
    lvg, lvb = row2d(ln_v_g[0]), row2d(ln_v_b[0])
    oa_p = _spatial_prompt(ug, w_s[0], bias_slab, lvg, lvb)
    oa_s, vn_s = _spatial_sample(ug, wexp, bias_slab, lvg, lvb)
    o_a = jnp.concatenate([oa_p, oa_s], axis=0)

    w_pool_b = w_pool[0].astype(BF16)
    pscale = row2d(pool_scale[0])
    buf_t = state_pool[0].transpose(1, 0, 2)
    ob_p = _pool_prompt(z, w_pool_b, pscale)
    ob_s = _pool_sample(z, buf_t, w_pool_b, pscale)
    o_b = jnp.concatenate([ob_p, ob_s], axis=0)

    m = _merge(o_a, o_b, w_up_a[0].astype(BF16), w_up_b[0].astype(BF16), gates)
    r1 = _out_proj(m, w_o[0].astype(BF16), x_prompt_rows, x_sample_rows)
    x1, x1b, x1pk, e_pad, gw_pad, rank_pad, counts = _ln_router(
        r1, row2d(ln_g[0, 0]), row2d(ln_b[0, 0]), w_router[0].astype(BF16), row2d(b_router[0]))

    item_e, item_start, item_n, sdst = _dispatch(e_pad, rank_pad, counts[0].astype(jnp.int32))
    y = _moe_routed(x1pk.reshape(T_ALL * PK_ROWS, LANES), w1[0], w3[0], w2[0], item_e, item_start, item_n, sdst)
    base = _shared_expert(x1, x1b, ws1[0].astype(BF16), ws3[0].astype(BF16), ws2[0].astype(BF16))
    gwb = jnp.broadcast_to(gw_pad[:, :TOP_K].reshape(T_ALL, TOP_K, 1), (T_ALL, TOP_K, LANES))
    slab = lambda v: v.reshape(1, SLAB_ROWS, LANES).astype(F32)
    x2, x2b = _combine(y.reshape(T_ALL, TOP_K, PK_ROWS, LANES), gwb, base, slab(ln_g[0, 1]), slab(ln_b[0, 1]))

    r3 = _ple(x2b, w_ple_gate[0].astype(BF16), pb, w_ple_proj[0].astype(BF16), x2)
    y_p, y_s = _ln_split(r3, row2d(ln_g[0, 2]), row2d(ln_b[0, 2]))

    y_prompt = y_p.reshape(BATCH, SEQ, D_MODEL)
    y_sample = y_s.reshape(DEC_SEQ, DEC_BATCH, D_MODEL).transpose(1, 0, 2)
    z_s = z[T_PROMPT:].reshape(DEC_SEQ, DEC_BATCH, CH_B).transpose(1, 0, 2)
    new_pool_prompt = jnp.stack([z[(b + 1) * SEQ - POOL_BUF:(b + 1) * SEQ] for b in range(BATCH)])[None]
    new_pool_sample = jnp.concatenate([state_pool[0][:, DEC_SEQ:], z_s], axis=1)[None]
    new_chunk_v_sample = vn_s.reshape(DEC_SEQ, DEC_BATCH, CH_A).transpose(1, 0, 2)[None]
    return (y_prompt, y_sample, new_pool_prompt, new_pool_sample, new_chunk_v_sample)
```
